```python
import jax, jax.numpy as jnp
from jax import lax
import numpy as np

D_MODEL = 1024
BATCH = 2
SEQ = 8192
DEPTH = 1

N_META = 16
ATTN_HEADS = 8
ATTN_HEAD_DIM = 64
ATTN_WIDTH = ATTN_HEADS * ATTN_HEAD_DIM
IDX_HEADS = 8
IDX_HEAD_DIM = 64
IDX_SCALE = (IDX_HEADS ** -0.5) * (IDX_HEAD_DIM ** -0.5)
ATTN_SCALE = ATTN_HEAD_DIM ** -0.5
TOPK_MAX = 256
Q_BLOCK = 128
HGRN_HEADS = 4
HGRN_KEY_DIM = 128
HGRN_VAL_DIM = 128
HGRN_WIDTH = HGRN_HEADS * HGRN_VAL_DIM
CHUNK = 64
MIX_WIDTH = ATTN_WIDTH + HGRN_WIDTH
IN_WIDTH = 4 * ATTN_WIDTH + IDX_HEADS * IDX_HEAD_DIM + IDX_HEAD_DIM + IDX_HEADS + 4 * HGRN_WIDTH
ROPE_THETA = 500000.0
ROPE_FRACTION = 4
EPS = 1e-6

kernel_name = "hymba_dsa_hgrn2_hybrid"


def rms_norm(x, g):
    xf = x.astype(jnp.float32)
    y = xf * lax.rsqrt(jnp.mean(xf * xf, axis=-1, keepdims=True) + EPS)
    return (y * g.astype(jnp.float32)).astype(x.dtype)


def layer_norm(x, g, b):
    xf = x.astype(jnp.float32)
    mu = jnp.mean(xf, axis=-1, keepdims=True)
    var = jnp.mean((xf - mu) ** 2, axis=-1, keepdims=True)
    y = (xf - mu) * lax.rsqrt(var + EPS)
    return (y * g.astype(jnp.float32) + b.astype(jnp.float32)).astype(x.dtype)


def rope_tables(T, head_dim):
    r = head_dim // ROPE_FRACTION
    half = r // 2
    inv = ROPE_THETA ** (-jnp.arange(half, dtype=jnp.float32) * 2.0 / r)
    ang = jnp.arange(T, dtype=jnp.float32)[:, None] * inv[None, :]
    return jnp.cos(ang)[:, None, :], jnp.sin(ang)[:, None, :]


def partial_rope(x, cos, sin):
    d = x.shape[-1]
    r = d // ROPE_FRACTION
    half = r // 2
    x1 = x[..., :half].astype(jnp.float32)
    x2 = x[..., half:r].astype(jnp.float32)
    rot = jnp.concatenate([x1 * cos - x2 * sin, x2 * cos + x1 * sin], axis=-1)
    return jnp.concatenate([rot.astype(x.dtype), x[..., r:]], axis=-1)


def split_columns(proj):
    sizes = [ATTN_WIDTH] * 4 + [IDX_HEADS * IDX_HEAD_DIM, IDX_HEAD_DIM, IDX_HEADS] + [HGRN_WIDTH] * 4
    points = np.cumsum(sizes)[:-1].tolist()
    return jnp.split(proj, points, axis=-1)


def sparse_attention(aq, ak, av, iq, ik, iw, idx_k_g, idx_k_b):
    B, T, _ = aq.shape
    S = T - N_META
    cos, sin = rope_tables(T, ATTN_HEAD_DIM)
    q = partial_rope(aq.reshape(B, T, ATTN_HEADS, ATTN_HEAD_DIM), cos, sin)
    k = partial_rope(ak.reshape(B, T, ATTN_HEADS, ATTN_HEAD_DIM), cos, sin)
    v = av.reshape(B, T, ATTN_HEADS, ATTN_HEAD_DIM)
    icos, isin = rope_tables(T, IDX_HEAD_DIM)
    qi = partial_rope(iq.reshape(B, T, IDX_HEADS, IDX_HEAD_DIM), icos, isin)
    ki = partial_rope(layer_norm(ik, idx_k_g, idx_k_b)[:, :, None, :], icos, isin)[:, :, 0, :]

    q_m, k_m, v_m = q[:, :N_META], k[:, :N_META], v[:, :N_META]
    s_m = jnp.einsum('bqhd,bkhd->bhqk', q_m, k_m).astype(jnp.float32) * ATTN_SCALE
    meta_causal = jnp.tril(jnp.ones((N_META, N_META), dtype=bool))
    s_m = jnp.where(meta_causal[None, None], s_m, -jnp.inf)
    p_m = jax.nn.softmax(s_m, axis=-1).astype(v.dtype)
    o_m = jnp.einsum('bhqk,bkhd->bqhd', p_m, v_m)

    q_r, k_r, v_r = q[:, N_META:], k[:, N_META:], v[:, N_META:]
    qi_r, ki_r, iw_r = qi[:, N_META:], ki[:, N_META:], iw[:, N_META:]
    n_blocks = S // Q_BLOCK
    topk = min(TOPK_MAX, S // 4)
    key_pos = jnp.arange(S)

    def to_blocks(a):
        return jnp.moveaxis(a.reshape((B, n_blocks, Q_BLOCK) + a.shape[2:]), 1, 0)

    def block_fn(args):
        blk, qb, qib, wb = args
        r = blk * Q_BLOCK + jnp.arange(Q_BLOCK)
        rel = jax.nn.relu(jnp.einsum('bqhd,bsd->bqhs', qib, ki_r))
        score = jnp.einsum('bqhs,bqh->bqs', rel, wb).astype(jnp.float32) * IDX_SCALE
        admissible = key_pos[None, :] <= r[:, None]
        score = jnp.where(admissible[None], score, -jnp.inf)
        _, sel = lax.top_k(score, topk)
        valid = sel <= r[None, :, None]
        k_sel = jax.vmap(lambda kb, ib: kb[ib])(k_r, sel)
        v_sel = jax.vmap(lambda vb, ib: vb[ib])(v_r, sel)
        s_sel = jnp.einsum('bqhd,bqkhd->bhqk', qb, k_sel).astype(jnp.float32) * ATTN_SCALE
        s_sel = jnp.where(valid[:, None], s_sel, -jnp.inf)
        s_meta = jnp.einsum('bqhd,bmhd->bhqm', qb, k_m).astype(jnp.float32) * ATTN_SCALE
        p = jax.nn.softmax(jnp.concatenate([s_meta, s_sel], axis=-1), axis=-1).astype(v.dtype)
        return (jnp.einsum('bhqm,bmhd->bqhd', p[..., :N_META], v_m)
                + jnp.einsum('bhqk,bqkhd->bqhd', p[..., N_META:], v_sel))

    o_r = lax.map(block_fn, (jnp.arange(n_blocks), to_blocks(q_r), to_blocks(qi_r), to_blocks(iw_r)))
    o_r = jnp.moveaxis(o_r, 0, 1).reshape(B, S, ATTN_HEADS, ATTN_HEAD_DIM)
    return jnp.concatenate([o_m, o_r], axis=1).reshape(B, T, ATTN_WIDTH)


def hgrn2(hq, hf, hi, lower_bound):
    B, T, _ = hq.shape
    f32 = jnp.float32
    q = jax.nn.silu(hq.astype(f32)).reshape(B, T, HGRN_HEADS, HGRN_KEY_DIM)
    f = lower_bound + (1.0 - lower_bound) * jax.nn.sigmoid(hf.astype(f32))
    log_f = jnp.log(f).reshape(B, T, HGRN_HEADS, HGRN_KEY_DIM)
    k = (1.0 - f).reshape(B, T, HGRN_HEADS, HGRN_KEY_DIM)
    v = hi.astype(f32).reshape(B, T, HGRN_HEADS, HGRN_VAL_DIM)
    pad = (-N_META) % CHUNK
    padw = ((0, 0), (pad, 0), (0, 0), (0, 0))
    q, log_f, k, v = [jnp.pad(a, padw) for a in (q, log_f, k, v)]
    Tp = T + pad
    n_chunks = Tp // CHUNK

    def chunks(a):
        return a.reshape(B, n_chunks, CHUNK, HGRN_HEADS, a.shape[-1]).transpose(1, 0, 3, 2, 4)

    causal = jnp.tril(jnp.ones((CHUNK, CHUNK), dtype=bool))[None, None, :, :, None]

    def step(state, xs):
        qc, kc, vc, gc = xs
        b = jnp.cumsum(gc, axis=2)
        diff = b[:, :, :, None, :] - b[:, :, None, :, :]
        decay = jnp.exp(jnp.where(causal, diff, -jnp.inf))
        a = jnp.einsum('bhtd,bhsd,bhtsd->bhts', qc, kc, decay)
        o = jnp.einsum('bhts,bhsv->bhtv', a, vc) + jnp.einsum('bhtd,bhdv->bhtv', qc * jnp.exp(b), state)
        b_last = b[:, :, -1]
        state = (jnp.exp(b_last)[..., None] * state
                 + jnp.einsum('bhsd,bhsv->bhdv', kc * jnp.exp(b_last[:, :, None, :] - b), vc))
        return state, o

    s0 = jnp.zeros((B, HGRN_HEADS, HGRN_KEY_DIM, HGRN_VAL_DIM), f32)
    _, o = lax.scan(step, s0, (chunks(q), chunks(k), chunks(v), chunks(log_f)))
    o = o.transpose(1, 0, 3, 2, 4).reshape(B, Tp, HGRN_HEADS, HGRN_VAL_DIM)
    return o[:, pad:].astype(hq.dtype)


def hybrid_mixer(h, w_in, idx_k_g, idx_k_b, attn_norm_g, lower_bound, hgrn_norm_g, w_out):
    B, T, _ = h.shape
    proj = jnp.einsum('btd,de->bte', h, w_in)
    aq, ak, av, ag, iq, ik, iw, hq, hf, hi, hg = split_columns(proj)
    o_attn = sparse_attention(aq, ak, av, iq, ik, iw, idx_k_g, idx_k_b)
    o_attn = rms_norm(o_attn, attn_norm_g) * jax.nn.silu(ag)
    o_hgrn = rms_norm(hgrn2(hq, hf, hi, lower_bound), hgrn_norm_g).reshape(B, T, HGRN_WIDTH)
    o_hgrn = o_hgrn * jax.nn.silu(hg)
    mixed = jnp.concatenate([o_attn, o_hgrn], axis=-1)
    return jnp.einsum('bte,ed->btd', mixed, w_out)


def setup_inputs(seed: int = 0) -> dict:
    key = jax.random.key(seed)
    ks = jax.random.split(key, 12)
    f32 = jnp.float32
    return {
        "x": jax.random.normal(ks[0], (BATCH, SEQ, D_MODEL), f32),
        "meta_tokens": jax.random.normal(ks[1], (N_META, D_MODEL), f32),
        "mix_norm": 1.0 + 0.05 * jax.random.normal(ks[2], (DEPTH, D_MODEL), f32),
        "w_in": jax.random.normal(ks[3], (DEPTH, D_MODEL, IN_WIDTH), f32) * D_MODEL ** -0.5,
        "idx_k_norm_g": 1.0 + 0.05 * jax.random.normal(ks[4], (DEPTH, IDX_HEAD_DIM), f32),
        "idx_k_norm_b": 0.02 * jax.random.normal(ks[5], (DEPTH, IDX_HEAD_DIM), f32),
        "attn_out_norm": 1.0 + 0.05 * jax.random.normal(ks[6], (DEPTH, ATTN_WIDTH), f32),
        "hgrn_lb_logits": 0.5 * jax.random.normal(ks[7], (DEPTH + 1, HGRN_WIDTH), f32),
        "hgrn_out_norm": 1.0 + 0.05 * jax.random.normal(ks[8], (DEPTH, HGRN_VAL_DIM), f32),
        "w_out": jax.random.normal(ks[9], (DEPTH, MIX_WIDTH, D_MODEL), f32) * MIX_WIDTH ** -0.5,
        "final_norm": 1.0 + 0.05 * jax.random.normal(ks[10], (D_MODEL,), f32),
    }


def reference(x, meta_tokens, mix_norm, w_in, idx_k_norm_g, idx_k_norm_b, attn_out_norm,
              hgrn_lb_logits, hgrn_out_norm, w_out, final_norm):
    B = x.shape[0]
    meta = jnp.broadcast_to(meta_tokens.astype(x.dtype)[None], (B, N_META, D_MODEL))
    h = jnp.concatenate([meta, x], axis=1)
    lower_bounds = jnp.cumsum(jax.nn.softmax(hgrn_lb_logits.astype(jnp.float32), axis=0), axis=0)
    for l in range(DEPTH):
        h = h + hybrid_mixer(rms_norm(h, mix_norm[l]), w_in[l], idx_k_norm_g[l], idx_k_norm_b[l],
                             attn_out_norm[l], lower_bounds[l], hgrn_out_norm[l], w_out[l])
    y = rms_norm(h, final_norm)
    return y[:, N_META:]
```

```python
import functools

import jax
import jax.numpy as jnp
import numpy as np
from jax import lax
from jax.experimental import pallas as pl
from jax.experimental.pallas import tpu as pltpu

N_META = 16
ATTN_HEADS = 8
HEAD_DIM = 64
ATTN_WIDTH = ATTN_HEADS * HEAD_DIM
IDX_HEADS = 8
IDX_DIM = 64
IDX_SCALE = (IDX_HEADS ** -0.5) * (IDX_DIM ** -0.5)
ATTN_SCALE = HEAD_DIM ** -0.5
TOPK_MAX = 256
HGRN_HEADS = 4
HGRN_DIM = 128
HGRN_WIDTH = HGRN_HEADS * HGRN_DIM
CHUNK = 64
SUB = 16
ROPE_THETA = 500000.0
ROPE_DIMS = HEAD_DIM // 4
ROPE_HALF = ROPE_DIMS // 2
EPS = 1e-6

LANES = 128
VMEM_LIMIT = 56 * 1024 * 1024

N_WIDE = 9
SMALL_W = 128
INT_MIN = -(2 ** 31)
INT_MAX = 2 ** 31 - 1
NEG_INF_BITS = -(2 ** 23)

f32 = jnp.float32
bf16 = jnp.bfloat16
i32 = jnp.int32


def _sigmoid(v):
    return 1.0 / (1.0 + jnp.exp(-v))


def _inproj_kernel(x_ref, g_ref, w_ref, cos_ref, sin_ref, lng_ref, lnb_ref, lbl_ref,
                   qa_ref, ka_ref, va_ref, ga_ref, qi_ref, ki_ref, wi_ref,
                   hq_ref, hlf_ref, hk_ref, hv_ref, hg_ref):
    x = x_ref[...]
    tm = x.shape[0]
    ms = jnp.mean(x * x, axis=-1, keepdims=True)
    xb = ((x * lax.rsqrt(ms + EPS)) * g_ref[...]).astype(bf16)

    cos1 = cos_ref[...]
    sin1 = sin_ref[...]
    cos = jnp.concatenate([cos1] * 4, axis=-1)
    sin = jnp.concatenate([sin1] * 4, axis=-1)
    lane = lax.broadcasted_iota(i32, (tm, ATTN_WIDTH), 1)
    low = (lane & (HEAD_DIM - 1)) < ROPE_HALF

    def proj(g):
        return jnp.dot(xb, w_ref[:, g * 512:(g + 1) * 512], preferred_element_type=f32)

    def rope(v):
        partner = jnp.where(low, pltpu.roll(v, ATTN_WIDTH - ROPE_HALF, 1),
                            pltpu.roll(v, ROPE_HALF, 1))
        return v * cos + partner * sin

    qa_ref[...] = (rope(proj(0)) * ATTN_SCALE).astype(bf16)
    ka_ref[...] = rope(proj(1)).astype(bf16)
    va_ref[...] = proj(2).astype(bf16)
    ag = proj(3)
    ga_ref[...] = ag * _sigmoid(ag)
    qi_ref[...] = rope(proj(4)).astype(bf16)

    hq = proj(5)
    hq_ref[...] = hq * _sigmoid(hq)
    lg = lbl_ref[...]
    e = jnp.exp(lg - jnp.max(lg, axis=0, keepdims=True))
    lb = e[0:1, :] / jnp.sum(e, axis=0, keepdims=True)
    f = lb + (1.0 - lb) * _sigmoid(proj(6))
    hlf_ref[...] = jnp.log(f)
    hk_ref[...] = 1.0 - f
    hv_ref[...] = proj(7)
    hg = proj(8)
    hg_ref[...] = hg * _sigmoid(hg)

    sm = jnp.dot(xb, w_ref[:, N_WIDE * 512:N_WIDE * 512 + SMALL_W], preferred_element_type=f32)
    lane1 = lax.broadcasted_iota(i32, (tm, SMALL_W), 1)
    is_key = lane1 < IDX_DIM
    mu = jnp.sum(jnp.where(is_key, sm, 0.0), axis=-1, keepdims=True) / IDX_DIM
    d = sm - mu
    var = jnp.sum(jnp.where(is_key, d * d, 0.0), axis=-1, keepdims=True) / IDX_DIM
    y = d * lax.rsqrt(var + EPS) * lng_ref[...] + lnb_ref[...]
    low1 = (lane1 & (IDX_DIM - 1)) < ROPE_HALF
    partner = jnp.where(low1, pltpu.roll(y, SMALL_W - ROPE_HALF, 1), pltpu.roll(y, ROPE_HALF, 1))
    y = y * cos1 + partner * sin1
    ki_ref[...] = y[:, :IDX_DIM].astype(bf16)
    wi_ref[...] = sm[:, IDX_DIM:IDX_DIM + IDX_HEADS] * IDX_SCALE


def _inproj(x2, g, w, cos_t, sin_t, lng, lnb, lbl, tm, n_pos_blocks):
    n = x2.shape[0]
    d_model = x2.shape[1]
    grid = (n // tm,)
    row = lambda i: (i, 0)
    const = lambda i: (0, 0)
    pos = lambda i: (i % n_pos_blocks, 0)
    wide_bf = jax.ShapeDtypeStruct((n, 512), bf16)
    wide_f = jax.ShapeDtypeStruct((n, 512), f32)
    out_shape = (wide_bf, wide_bf, wide_bf, wide_f, wide_bf,
                 jax.ShapeDtypeStruct((n, IDX_DIM), bf16),
                 jax.ShapeDtypeStruct((n, IDX_HEADS), f32),
                 wide_f, wide_f, wide_f, wide_f, wide_f)
    wide_spec = pl.BlockSpec((tm, 512), row)
    out_specs = (wide_spec,) * 5 + (pl.BlockSpec((tm, IDX_DIM), row),
                                    pl.BlockSpec((tm, IDX_HEADS), row)) + (wide_spec,) * 5
    return pl.pallas_call(
        _inproj_kernel,
        grid=grid,
        in_specs=[
            pl.BlockSpec((tm, d_model), row),
            pl.BlockSpec((1, d_model), const),
            pl.BlockSpec(w.shape, const),
            pl.BlockSpec((tm, LANES), pos),
            pl.BlockSpec((tm, LANES), pos),
            pl.BlockSpec((1, SMALL_W), const),
            pl.BlockSpec((1, SMALL_W), const),
            pl.BlockSpec(lbl.shape, const),
        ],
        out_specs=out_specs,
        out_shape=out_shape,
        compiler_params=pltpu.CompilerParams(
            dimension_semantics=("parallel",), vmem_limit_bytes=VMEM_LIMIT),
        name="inproj",
    )(x2, g, w, cos_t, sin_t, lng, lnb, lbl)


def _sortable(v):
    bits = lax.bitcast_convert_type(v + 0.0, i32)
    return bits ^ ((bits >> 31) & INT_MAX)


def _attn_kernel(qa_ref, qi_ref, wi_ref, ki_ref, ka_ref, va_ref, km_ref, vm_ref, o_ref,
                 sc_ref, thr_ref, tie_ref, *, tq, tk, rg, topk):
    i = pl.program_id(1)
    nkb = (i * tq + tq - 1) // tk + 1
    n_lane_chunks = tk // LANES
    n_rg = tq // rg
    col_bits = max(1, (sc_ref.shape[0] * tk - 1).bit_length())

    qi = qi_ref[...]
    wi = wi_ref[...]
    qi_heads = [qi[:, h * IDX_DIM:(h + 1) * IDX_DIM] for h in range(IDX_HEADS)]
    w_heads = [wi[:, h:h + 1] for h in range(IDX_HEADS)]
    row_g = i * tq + lax.broadcasted_iota(i32, (tq, tk), 0)
    col_l = lax.broadcasted_iota(i32, (tq, tk), 1)

    def score_body(kb, carry):
        kik = ki_ref[pl.ds(pl.multiple_of(kb * tk, tk), tk), :]
        acc = jnp.zeros((tq, tk), f32)
        for h in range(IDX_HEADS):
            s = lax.dot_general(qi_heads[h], kik, (((1,), (1,)), ((), ())),
                                preferred_element_type=f32)
            acc = acc + jnp.maximum(s, 0.0) * w_heads[h]
        acc = jnp.where(kb * tk + col_l <= row_g, acc, -jnp.inf)
        sc_ref[kb] = _sortable(acc)
        return carry

    lax.fori_loop(0, nkb, score_body, 0)

    def count(rows, pred):
        def body(kb, acc):
            for c in range(n_lane_chunks):
                v = sc_ref[kb, rows, c * LANES:(c + 1) * LANES]
                colg = kb * tk + c * LANES + lax.broadcasted_iota(i32, (rg, LANES), 1)
                acc = acc + jnp.where(pred(v, colg), 1, 0)
            return acc
        acc = lax.fori_loop(0, nkb, body, jnp.zeros((rg, LANES), i32))
        return jnp.sum(acc, axis=1, keepdims=True)

    for r in range(n_rg):
        rows = slice(r * rg, (r + 1) * rg)
        c_all = jnp.full((rg, 1), 1, i32) * (nkb * tk)

        def bit_body(j, carry):
            t, c_t = carry
            cand = jnp.where(j == 0, jnp.zeros_like(t), t | (1 << jnp.maximum(31 - j, 0)))
            cb = jnp.broadcast_to(cand, (rg, LANES))
            c = count(rows, lambda v, colg: v >= cb)
            ok = c >= topk
            return jnp.where(ok, cand, t), jnp.where(ok, c, c_t)

        t, c_ge = lax.fori_loop(0, 32, bit_body, (jnp.full((rg, 1), INT_MIN, i32), c_all))
        thr_ref[rows, :] = jnp.broadcast_to(t, (rg, LANES))
        tie_ref[rows, :] = jnp.full((rg, LANES), INT_MAX, i32)

        @pl.when(jnp.max(c_ge) > topk)
        def _():
            tb = jnp.broadcast_to(t, (rg, LANES))
            c_gt = count(rows, lambda v, colg: v > tb)
            need = topk - c_gt

            def idx_body(j, jcol):
                cand = jcol | (1 << (col_bits - 1 - j))
                cb = jnp.broadcast_to(cand, (rg, LANES))
                c = count(rows, lambda v, colg: (v == tb) & (colg < cb))
                return jnp.where(c < need, cand, jcol)

            jcol = lax.fori_loop(0, col_bits, idx_body, jnp.zeros((rg, 1), i32))
            jcol = jnp.where(c_ge > topk, jcol, INT_MAX)
            tie_ref[rows, :] = jnp.broadcast_to(jcol, (rg, LANES))

    thr = thr_ref[...][:, 0:1]
    tie = tie_ref[...][:, 0:1]

    def mask_body(kb, carry):
        v = sc_ref[kb]
        colg = kb * tk + col_l
        sel = ((v > thr) | ((v == thr) & (colg <= tie))) & (colg <= row_g)
        sc_ref[kb] = jnp.where(sel, 0, NEG_INF_BITS)
        return carry

    lax.fori_loop(0, nkb, mask_body, 0)

    lane_q = lax.broadcasted_iota(i32, (tq, LANES), 1)
    first = lane_q < HEAD_DIM
    outs = []
    for g in range(ATTN_HEADS // 2):
        cols = slice(g * LANES, (g + 1) * LANES)
        qg = qa_ref[:, cols]
        q_pair = (jnp.where(first, qg, jnp.zeros_like(qg)), jnp.where(first, jnp.zeros_like(qg), qg))
        km = km_ref[:, cols]
        vm = vm_ref[:, cols]
        m0, l0, a0 = [], [], []
        for hh in range(2):
            s = lax.dot_general(q_pair[hh], km, (((1,), (1,)), ((), ())), preferred_element_type=f32)
            m = jnp.max(s, axis=1, keepdims=True)
            p = jnp.exp(s - m)
            m0.append(m)
            l0.append(jnp.sum(p, axis=1, keepdims=True))
            a0.append(jnp.dot(p.astype(bf16), vm, preferred_element_type=f32))
        acc0 = jnp.where(first, a0[0], a0[1])

        def attn_body(kb, carry):
            m_a, m_b, l_a, l_b, acc = carry
            r0 = pl.multiple_of(kb * tk, tk)
            kt = ka_ref[pl.ds(r0, tk), cols]
            vt = va_ref[pl.ds(r0, tk), cols]
            bias = lax.bitcast_convert_type(sc_ref[kb], f32)
            new = []
            for hh, (m_old, l_old) in enumerate(((m_a, l_a), (m_b, l_b))):
                s = lax.dot_general(q_pair[hh], kt, (((1,), (1,)), ((), ())),
                                    preferred_element_type=f32) + bias
                m_new = jnp.maximum(m_old, jnp.max(s, axis=1, keepdims=True))
                p = jnp.exp(s - m_new)
                alpha = jnp.exp(m_old - m_new)
                l_new = alpha * l_old + jnp.sum(p, axis=1, keepdims=True)
                pv = jnp.dot(p.astype(bf16), vt, preferred_element_type=f32)
                new.append((m_new, l_new, alpha, pv))
            alpha = jnp.where(first, new[0][2], new[1][2])
            acc = alpha * acc + jnp.where(first, new[0][3], new[1][3])
            return new[0][0], new[1][0], new[0][1], new[1][1], acc

        m_a, m_b, l_a, l_b, acc = lax.fori_loop(0, nkb, attn_body, (m0[0], m0[1], l0[0], l0[1], acc0))
        outs.append(acc / jnp.where(first, l_a, l_b))
    o_ref[...] = jnp.concatenate(outs, axis=-1)


def _sparse_attention(qa, qi, wi, ki, ka, va, km, vm, batch, seq, tq, tk, rg):
    n = qa.shape[0]
    nq = seq // tq
    topk = min(TOPK_MAX, seq // 4)
    qrow = lambda b, i: (b * nq + i, 0)
    brow = lambda b, i: (b, 0)
    const = lambda b, i: (0, 0)
    kern = functools.partial(_attn_kernel, tq=tq, tk=tk, rg=rg, topk=topk)
    return pl.pallas_call(
        kern,
        grid=(batch, nq),
        in_specs=[
            pl.BlockSpec((tq, ATTN_WIDTH), qrow),
            pl.BlockSpec((tq, ATTN_WIDTH), qrow),
            pl.BlockSpec((tq, IDX_HEADS), qrow),
            pl.BlockSpec((seq, IDX_DIM), brow),
            pl.BlockSpec((seq, ATTN_WIDTH), brow),
            pl.BlockSpec((seq, ATTN_WIDTH), brow),
            pl.BlockSpec((N_META, ATTN_WIDTH), const),
            pl.BlockSpec((N_META, ATTN_WIDTH), const),
        ],
        out_specs=pl.BlockSpec((tq, ATTN_WIDTH), qrow),
        out_shape=jax.ShapeDtypeStruct((n, ATTN_WIDTH), f32),
        scratch_shapes=[
            pltpu.VMEM((seq // tk, tq, tk), i32),
            pltpu.VMEM((tq, LANES), i32),
            pltpu.VMEM((tq, LANES), i32),
        ],
        compiler_params=pltpu.CompilerParams(
            dimension_semantics=("parallel", "arbitrary"), vmem_limit_bytes=VMEM_LIMIT),
        name="sparse_attn",
    )(qa, qi, wi, ki, ka, va, km, vm)


def _hgrn_chunk(q, lf, k, v, st, tri, band_masks, blk_lower):
    b = jnp.dot(tri, lf, preferred_element_type=f32, precision=lax.Precision.HIGHEST)
    bl = b[CHUNK - 1:CHUNK, :]
    q_in = (q * jnp.exp(b)).astype(bf16)
    o = lax.dot_general(q_in, st.astype(bf16), (((1,), (1,)), ((), ())), preferred_element_type=f32)

    blocks = [jnp.zeros((SUB, CHUNK), f32)]
    for si in range(1, CHUNK // SUB):
        bref = b[si * SUB - 1:si * SUB, :]
        rows = slice(si * SUB, (si + 1) * SUB)
        q_s = (q[rows] * jnp.exp(b[rows] - bref)).astype(bf16)
        k_s = (k * jnp.exp(jnp.minimum(bref - b, 0.0))).astype(bf16)
        blocks.append(lax.dot_general(q_s, k_s, (((1,), (1,)), ((), ())), preferred_element_type=f32))
    a = jnp.where(blk_lower, jnp.concatenate(blocks, axis=0), 0.0)

    for dlt in range(SUB):
        if dlt == 0:
            w = q * k
        else:
            decay = jnp.exp(jnp.minimum(b - pltpu.roll(b, dlt, 0), 0.0))
            w = q * pltpu.roll(k, dlt, 0) * decay
        a = a + jnp.where(band_masks[dlt], jnp.sum(w, axis=1, keepdims=True), 0.0)

    o = o + jnp.dot(a.astype(bf16), v.astype(bf16), preferred_element_type=f32)
    k_end = (k * jnp.exp(bl - b)).astype(bf16)
    upd = lax.dot_general(v.astype(bf16), k_end, (((0,), (0,)), ((), ())), preferred_element_type=f32)
    return o, st * jnp.exp(bl) + upd


def _hgrn_kernel(q_ref, lf_ref, k_ref, v_ref, lfm_ref, km_ref, vm_ref, o_ref, st_ref, *, n_chunks):
    c = pl.program_id(1)
    r_i = lax.broadcasted_iota(i32, (CHUNK, CHUNK), 0)
    c_i = lax.broadcasted_iota(i32, (CHUNK, CHUNK), 1)
    tri = jnp.where(c_i <= r_i, 1.0, 0.0).astype(f32)
    sub_shift = SUB.bit_length() - 1
    blk_lower = (c_i >> sub_shift) < (r_i >> sub_shift)
    band_masks = [((r_i & (SUB - 1)) >= d) & (c_i == r_i - d) for d in range(SUB)]

    @pl.when(c == 0)
    def _():
        pad = jnp.zeros((CHUNK - N_META, HGRN_DIM), f32)
        for h in range(HGRN_HEADS):
            cols = slice(h * HGRN_DIM, (h + 1) * HGRN_DIM)
            lf = jnp.concatenate([pad, lfm_ref[:, cols]], axis=0)
            k = jnp.concatenate([pad, km_ref[:, cols]], axis=0)
            v = jnp.concatenate([pad, vm_ref[:, cols]], axis=0)
            _, st = _hgrn_chunk(jnp.zeros((CHUNK, HGRN_DIM), f32), lf, k, v,
                                jnp.zeros((HGRN_DIM, HGRN_DIM), f32), tri, band_masks, blk_lower)
            st_ref[h] = st

    def chunk_body(ci, carry):
        rows = pl.ds(pl.multiple_of(ci * CHUNK, CHUNK), CHUNK)
        for h in range(HGRN_HEADS):
            cols = slice(h * HGRN_DIM, (h + 1) * HGRN_DIM)
            o, st = _hgrn_chunk(q_ref[rows, cols], lf_ref[rows, cols], k_ref[rows, cols],
                                v_ref[rows, cols], st_ref[h], tri, band_masks, blk_lower)
            o_ref[rows, cols] = o
            st_ref[h] = st
        return carry

    lax.fori_loop(0, n_chunks, chunk_body, 0)


def _hgrn(hq, hlf, hk, hv, lfm, km, vm, batch, seq, tc):
    n = hq.shape[0]
    nc = seq // tc
    row = lambda b, c: (b * nc + c, 0)
    const = lambda b, c: (0, 0)
    spec = pl.BlockSpec((tc, HGRN_WIDTH), row)
    mspec = pl.BlockSpec((N_META, HGRN_WIDTH), const)
    return pl.pallas_call(
        functools.partial(_hgrn_kernel, n_chunks=tc // CHUNK),
        grid=(batch, nc),
        in_specs=[spec, spec, spec, spec, mspec, mspec, mspec],
        out_specs=spec,
        out_shape=jax.ShapeDtypeStruct((n, HGRN_WIDTH), f32),
        scratch_shapes=[pltpu.VMEM((HGRN_HEADS, HGRN_DIM, HGRN_DIM), f32)],
        compiler_params=pltpu.CompilerParams(
            dimension_semantics=("parallel", "arbitrary"), vmem_limit_bytes=VMEM_LIMIT),
        name="hgrn2",
    )(hq, hlf, hk, hv, lfm, km, vm)


def _outproj_kernel(oa_ref, ga_ref, oh_ref, hg_ref, x_ref, w_ref, na_ref, nh_ref, nf_ref, y_ref):
    oa = oa_ref[...]
    a = oa * lax.rsqrt(jnp.mean(oa * oa, axis=-1, keepdims=True) + EPS) * na_ref[...] * ga_ref[...]
    parts = []
    for h in range(HGRN_HEADS):
        cols = slice(h * HGRN_DIM, (h + 1) * HGRN_DIM)
        oh = oh_ref[:, cols]
        nrm = oh * lax.rsqrt(jnp.mean(oh * oh, axis=-1, keepdims=True) + EPS) * nh_ref[...]
        parts.append(nrm * hg_ref[:, cols])
    hh = jnp.concatenate(parts, axis=-1)
    mix = (jnp.dot(a.astype(bf16), w_ref[:ATTN_WIDTH, :], preferred_element_type=f32)
           + jnp.dot(hh.astype(bf16), w_ref[ATTN_WIDTH:, :], preferred_element_type=f32))
    hres = x_ref[...] + mix
    y_ref[...] = hres * lax.rsqrt(jnp.mean(hres * hres, axis=-1, keepdims=True) + EPS) * nf_ref[...]


def _outproj(oa, ga, oh, hg, x2, w, na, nh, nf, tm):
    n, d_model = x2.shape
    row = lambda i: (i, 0)
    const = lambda i: (0, 0)
    half = pl.BlockSpec((tm, 512), row)
    return pl.pallas_call(
        _outproj_kernel,
        grid=(n // tm,),
        in_specs=[half, half, half, half,
                  pl.BlockSpec((tm, d_model), row),
                  pl.BlockSpec(w.shape, const),
                  pl.BlockSpec((1, ATTN_WIDTH), const),
                  pl.BlockSpec((1, HGRN_DIM), const),
                  pl.BlockSpec((1, d_model), const)],
        out_specs=pl.BlockSpec((tm, d_model), row),
        out_shape=jax.ShapeDtypeStruct((n, d_model), f32),
        compiler_params=pltpu.CompilerParams(
            dimension_semantics=("parallel",), vmem_limit_bytes=VMEM_LIMIT),
        name="outproj",
    )(oa, ga, oh, hg, x2, w, na, nh, nf)


def _rope_tables(n_pos):
    inv = ROPE_THETA ** (-jnp.arange(ROPE_HALF, dtype=f32) * 2.0 / ROPE_DIMS)
    ang = jnp.arange(n_pos, dtype=f32)[:, None] * inv[None, :]
    cos, sin = jnp.cos(ang), jnp.sin(ang)
    rest = HEAD_DIM - ROPE_DIMS
    cos_h = jnp.concatenate([cos, cos, jnp.ones((n_pos, rest), f32)], axis=-1)
    sin_h = jnp.concatenate([-sin, sin, jnp.zeros((n_pos, rest), f32)], axis=-1)
    return jnp.tile(cos_h, (1, LANES // HEAD_DIM)), jnp.tile(sin_h, (1, LANES // HEAD_DIM))


def kernel(x, meta_tokens, mix_norm, w_in, idx_k_norm_g, idx_k_norm_b, attn_out_norm,
           hgrn_lb_logits, hgrn_out_norm, w_out, final_norm):
    batch, seq, d_model = x.shape
    x2 = x.reshape(batch * seq, d_model)

    wl = w_in[0]
    a_end = 4 * ATTN_WIDTH
    iq_end = a_end + IDX_HEADS * IDX_DIM
    sm_end = iq_end + IDX_DIM + IDX_HEADS
    w_small = jnp.pad(wl[:, iq_end:sm_end], ((0, 0), (0, SMALL_W - (sm_end - iq_end))))
    w = jnp.concatenate([wl[:, :iq_end], wl[:, sm_end:], w_small], axis=-1).astype(bf16)

    cos_t, sin_t = _rope_tables(N_META + seq)
    g = mix_norm[0][None, :]
    lng = jnp.pad(idx_k_norm_g[0], (0, SMALL_W - IDX_DIM))[None, :]
    lnb = jnp.pad(idx_k_norm_b[0], (0, SMALL_W - IDX_DIM))[None, :]

    tm = 256
    real = _inproj(x2, g, w, cos_t[N_META:], sin_t[N_META:], lng, lnb, hgrn_lb_logits,
                   tm, seq // tm)
    meta = _inproj(meta_tokens, g, w, cos_t[:N_META], sin_t[:N_META], lng, lnb, hgrn_lb_logits,
                   N_META, 1)
    qa, ka, va, ga, qi, ki, wi, hq, hlf, hk, hv, hg = real
    ka_m, va_m = meta[1], meta[2]
    hlf_m, hk_m, hv_m = meta[8], meta[9], meta[10]

    oa = _sparse_attention(qa, qi, wi, ki, ka, va, ka_m, va_m, batch, seq, tq=256, tk=512, rg=64)
    oh = _hgrn(hq, hlf, hk, hv, hlf_m, hk_m, hv_m, batch, seq, tc=512)
    y = _outproj(oa, ga, oh, hg, x2, w_out[0].astype(bf16), attn_out_norm[0][None, :],
                 hgrn_out_norm[0][None, :], final_norm[None, :], tm=256)
    return y.reshape(batch, seq, d_model)
```

```python
import functools
import math

import jax
import jax.numpy as jnp
from jax import lax
from jax.experimental import pallas as pl
from jax.experimental.pallas import tpu as pltpu

N_META = 16
ATTN_HEADS = 8
HEAD_DIM = 64
ATTN_WIDTH = ATTN_HEADS * HEAD_DIM
IDX_HEADS = 8
IDX_DIM = 64
IDX_SCALE = (IDX_HEADS ** -0.5) * (IDX_DIM ** -0.5)
ATTN_SCALE = HEAD_DIM ** -0.5
LOG2E = math.log2(math.e)
TOPK_MAX = 256
HGRN_HEADS = 4
HGRN_DIM = 128
HGRN_WIDTH = HGRN_HEADS * HGRN_DIM
CHUNK = 64
SUB = 16
ROPE_THETA = 500000.0
ROPE_DIMS = HEAD_DIM // 4
ROPE_HALF = ROPE_DIMS // 2
EPS = 1e-6

LANES = 128
SUBLANES = 8
VMEM_LIMIT = 56 * 1024 * 1024

TILE = 256
N_WIDE = 9
SMALL_W = 128
ONES_ROWS = 16
VT_ROWS = HEAD_DIM + ONES_ROWS
INT_MAX = 2 ** 31 - 1
KEY_NEG_INF = -(2 ** 23) ^ INT_MAX
KEY_POS_INF = 0x7F800000
FLT_LOWEST = -3.4028234663852886e38
MAX_BISECT = 34

f32 = jnp.float32
bf16 = jnp.bfloat16
i32 = jnp.int32


def _sigmoid(v):
    return 1.0 / (1.0 + jnp.exp(-v))


def _normed_rows(x_ref, g_ref):
    x = x_ref[...]
    ms = jnp.mean(x * x, axis=-1, keepdims=True)
    return ((x * lax.rsqrt(ms + EPS)) * g_ref[...]).astype(bf16)


def _rope_fn(cos1, sin1, tm):
    cos = jnp.concatenate([cos1] * 4, axis=-1)
    sin = jnp.concatenate([sin1] * 4, axis=-1)
    lane = lax.broadcasted_iota(i32, (tm, ATTN_WIDTH), 1)
    low = (lane & (HEAD_DIM - 1)) < ROPE_HALF

    def rope(v):
        partner = jnp.where(low, pltpu.roll(v, ATTN_WIDTH - ROPE_HALF, 1),
                            pltpu.roll(v, ROPE_HALF, 1))
        return v * cos + partner * sin
    return rope


def _hgrn_gates(lbl_ref, hf):
    lg = lbl_ref[...]
    e = jnp.exp(lg - jnp.max(lg, axis=0, keepdims=True))
    lb = e[0:1, :] / jnp.sum(e, axis=0, keepdims=True)
    f = lb + (1.0 - lb) * _sigmoid(hf)
    return jnp.log(f), 1.0 - f


def _inproj_kernel(x_ref, g_ref, w_ref, cos_ref, sin_ref, lng_ref, lnb_ref, lbl_ref,
                   qat_ref, ka_ref, vt_ref, ga_ref, qit_ref, ki_ref, wt_ref,
                   hq_ref, hlf_ref, hk_ref, hv_ref, hg_ref):
    xb = _normed_rows(x_ref, g_ref)
    tm = xb.shape[0]
    cos1 = cos_ref[...]
    sin1 = sin_ref[...]
    rope = _rope_fn(cos1, sin1, tm)

    def proj(g):
        return jnp.dot(xb, w_ref[:, g * 512:(g + 1) * 512], preferred_element_type=f32)

    qat_ref[...] = (rope(proj(0)) * (ATTN_SCALE * LOG2E)).T.astype(bf16)
    ka_ref[...] = rope(proj(1)).astype(bf16)
    vt = proj(2).T
    ones = jnp.ones((ONES_ROWS, tm), f32)
    parts = []
    for h in range(ATTN_HEADS):
        parts += [vt[h * HEAD_DIM:(h + 1) * HEAD_DIM], ones]
    vt_ref[0] = jnp.concatenate(parts, axis=0).astype(bf16)
    ag = proj(3)
    ga_ref[...] = ag * _sigmoid(ag)
    qit_ref[...] = rope(proj(4)).T.astype(bf16)

    hq = proj(5)
    hq_ref[...] = hq * _sigmoid(hq)
    hlf_ref[...], hk_ref[...] = _hgrn_gates(lbl_ref, proj(6))
    hv_ref[...] = proj(7)
    hg = proj(8)
    hg_ref[...] = hg * _sigmoid(hg)

    sm = jnp.dot(xb, w_ref[:, N_WIDE * 512:N_WIDE * 512 + SMALL_W], preferred_element_type=f32)
    lane1 = lax.broadcasted_iota(i32, (tm, SMALL_W), 1)
    is_key = lane1 < IDX_DIM
    mu = jnp.sum(jnp.where(is_key, sm, 0.0), axis=-1, keepdims=True) / IDX_DIM
    d = sm - mu
    var = jnp.sum(jnp.where(is_key, d * d, 0.0), axis=-1, keepdims=True) / IDX_DIM
    y = d * lax.rsqrt(var + EPS) * lng_ref[...] + lnb_ref[...]
    low1 = (lane1 & (IDX_DIM - 1)) < ROPE_HALF
    partner = jnp.where(low1, pltpu.roll(y, SMALL_W - ROPE_HALF, 1), pltpu.roll(y, ROPE_HALF, 1))
    y = y * cos1 + partner * sin1
    ki_ref[...] = y[:, :IDX_DIM].astype(bf16)
    wt_ref[...] = sm.T[IDX_DIM:IDX_DIM + IDX_HEADS] * IDX_SCALE


def _inproj(x2, g, w, cos_t, sin_t, lng, lnb, lbl, n_pos_blocks):
    n, d_model = x2.shape
    tm = TILE
    row = lambda i: (i, 0)
    col = lambda i: (0, i)
    const = lambda i: (0, 0)
    pos = lambda i: (i % n_pos_blocks, 0)
    wide_bf = jax.ShapeDtypeStruct((n, 512), bf16)
    wide_f = jax.ShapeDtypeStruct((n, 512), f32)
    wide_t = jax.ShapeDtypeStruct((512, n), bf16)
    out_shape = (wide_t, wide_bf,
                 jax.ShapeDtypeStruct((n // tm, ATTN_HEADS * VT_ROWS, tm), bf16),
                 wide_f, wide_t,
                 jax.ShapeDtypeStruct((n, IDX_DIM), bf16),
                 jax.ShapeDtypeStruct((IDX_HEADS, n), f32),
                 wide_f, wide_f, wide_f, wide_f, wide_f)
    wide_spec = pl.BlockSpec((tm, 512), row)
    t_spec = pl.BlockSpec((512, tm), col)
    out_specs = (t_spec, wide_spec,
                 pl.BlockSpec((1, ATTN_HEADS * VT_ROWS, tm), lambda i: (i, 0, 0)),
                 wide_spec, t_spec,
                 pl.BlockSpec((tm, IDX_DIM), row),
                 pl.BlockSpec((IDX_HEADS, tm), col)) + (wide_spec,) * 5
    return pl.pallas_call(
        _inproj_kernel,
        grid=(n // tm,),
        in_specs=[
            pl.BlockSpec((tm, d_model), row),
            pl.BlockSpec((1, d_model), const),
            pl.BlockSpec(w.shape, const),
            pl.BlockSpec((tm, LANES), pos),
            pl.BlockSpec((tm, LANES), pos),
            pl.BlockSpec((1, SMALL_W), const),
            pl.BlockSpec((1, SMALL_W), const),
            pl.BlockSpec(lbl.shape, const),
        ],
        out_specs=out_specs,
        out_shape=out_shape,
        compiler_params=pltpu.CompilerParams(
            dimension_semantics=("parallel",), vmem_limit_bytes=VMEM_LIMIT),
        name="inproj",
    )(x2, g, w, cos_t, sin_t, lng, lnb, lbl)


def _meta_proj_kernel(x_ref, g_ref, w_ref, cos_ref, sin_ref, lbl_ref,
                      ka_ref, va_ref, hlf_ref, hk_ref, hv_ref):
    xb = _normed_rows(x_ref, g_ref)
    rope = _rope_fn(cos_ref[...], sin_ref[...], xb.shape[0])

    def proj(g):
        return jnp.dot(xb, w_ref[:, g * 512:(g + 1) * 512], preferred_element_type=f32)

    ka_ref[...] = rope(proj(1)).astype(bf16)
    va_ref[...] = proj(2).astype(bf16)
    hlf_ref[...], hk_ref[...] = _hgrn_gates(lbl_ref, proj(6))
    hv_ref[...] = proj(7)


def _meta_proj(meta, g, w, cos_t, sin_t, lbl):
    n, d_model = meta.shape
    full = lambda a: pl.BlockSpec(a.shape, lambda i: (0,) * a.ndim)
    wide = pl.BlockSpec((n, 512), lambda i: (0, 0))
    return pl.pallas_call(
        _meta_proj_kernel,
        grid=(1,),
        in_specs=[full(meta), full(g), full(w), full(cos_t), full(sin_t), full(lbl)],
        out_specs=(wide,) * 5,
        out_shape=(jax.ShapeDtypeStruct((n, 512), bf16),) * 2
        + (jax.ShapeDtypeStruct((n, 512), f32),) * 3,
        compiler_params=pltpu.CompilerParams(vmem_limit_bytes=VMEM_LIMIT),
        name="meta_proj",
    )(meta, g, w, cos_t, sin_t, lbl)


def _unkey(k):
    return lax.bitcast_convert_type(k ^ ((k >> 31) & INT_MAX), f32)


def _attn_kernel(qat_ref, qit_ref, wt_ref, ki_ref, ka_ref, vt_ref, km_ref, vm_ref, o_ref,
                 sc_ref, s_ref, acc_ref, m_ref, thr_ref, *, t, topk):
    i = pl.program_id(1)
    nkb = i + 1
    n_grp = t // LANES
    col_bits = max(1, (sc_ref.shape[0] * t - 1).bit_length())
    key_l = lax.broadcasted_iota(i32, (t, t), 0)
    qry_l = lax.broadcasted_iota(i32, (t, t), 1)

    def score_body(kb, carry):
        kik = ki_ref[pl.ds(pl.multiple_of(kb * t, t), t), :]
        acc = jnp.zeros((t, t), f32)
        for h in range(IDX_HEADS):
            s = jnp.dot(kik, qit_ref[h * IDX_DIM:(h + 1) * IDX_DIM, :], preferred_element_type=f32)
            acc = acc + jnp.maximum(s, 0.0) * wt_ref[h:h + 1, :]
        sc_ref[kb] = acc
        return carry

    lax.fori_loop(0, nkb, score_body, 0)
    sc_ref[i] = jnp.where(key_l <= qry_l, sc_ref[i], -jnp.inf)

    sub_l = lax.broadcasted_iota(i32, (SUBLANES, LANES), 0)
    for g in range(n_grp):
        lanes = slice(g * LANES, (g + 1) * LANES)

        def count(pred):
            def body(kb, accs):
                accs = list(accs)
                for r in range(t // SUBLANES):
                    v = sc_ref[kb, r * SUBLANES:(r + 1) * SUBLANES, lanes]
                    accs[r % 4] = accs[r % 4] + jnp.where(pred(v, kb * t + r * SUBLANES + sub_l), 1, 0)
                return tuple(accs)
            z = jnp.zeros((SUBLANES, LANES), i32)
            a = lax.fori_loop(0, nkb, body, (z, z, z, z))
            return jnp.sum((a[0] + a[1]) + (a[2] + a[3]), axis=0, keepdims=True)

        def cond(carry):
            it, lo, hi, c_lo, c_hi = carry
            open_ = jnp.where((c_lo != topk) & (hi != lo + 1), 1, 0)
            return (it < MAX_BISECT) & (jnp.max(open_) > 0)

        def step(carry):
            it, lo, hi, c_lo, c_hi = carry
            mid = lo + lax.shift_right_logical(hi - lo, 1)
            cb = jnp.broadcast_to(_unkey(mid), (SUBLANES, LANES))
            c = count(lambda v, idx: v >= cb)
            ok = c >= topk
            return (it + 1, jnp.where(ok, mid, lo), jnp.where(ok, hi, mid),
                    jnp.where(ok, c, c_lo), jnp.where(ok, c_hi, c))

        row = lambda val: jnp.full((1, LANES), val, i32)
        _, lo, hi, c_lo, c_hi = lax.while_loop(
            cond, step, (jnp.int32(0), row(KEY_NEG_INF), row(KEY_POS_INF), row(0) + nkb * t, row(0)))
        thr = _unkey(lo)
        thr_ref[:, lanes] = jnp.broadcast_to(jnp.maximum(thr, FLT_LOWEST), (SUBLANES, LANES))

        @pl.when(jnp.max(c_lo) > topk)
        def _():
            tb = jnp.broadcast_to(thr, (SUBLANES, LANES))
            need = topk - c_hi

            def idx_body(j, jcol):
                cand = jcol | (1 << (col_bits - 1 - j))
                cb = jnp.broadcast_to(cand, (SUBLANES, LANES))
                c = count(lambda v, idx: (v == tb) & (idx < cb))
                return jnp.where(c < need, cand, jcol)

            jcol = lax.fori_loop(0, col_bits, idx_body, jnp.zeros((1, LANES), i32))
            jcol = jnp.where(c_lo > topk, jcol, INT_MAX)

            def drop_body(kb, carry):
                v = sc_ref[kb, :, lanes]
                idx = kb * t + lax.broadcasted_iota(i32, (t, LANES), 0)
                sc_ref[kb, :, lanes] = jnp.where((v == thr) & (idx > jcol), -jnp.inf, v)
                return carry

            lax.fori_loop(0, nkb, drop_body, 0)

    row_h = lax.broadcasted_iota(i32, (2 * HEAD_DIM, t), 0)
    first = row_h < HEAD_DIM

    def q_head(h):
        qp = qat_ref[(h // 2) * LANES:(h // 2 + 1) * LANES, :]
        return jnp.where(first if h % 2 == 0 else ~first, qp, jnp.zeros_like(qp))

    for h in range(ATTN_HEADS):
        cols = slice((h // 2) * LANES, (h // 2 + 1) * LANES)
        s = jnp.dot(km_ref[:, cols], q_head(h), preferred_element_type=f32)
        m0 = jnp.max(s, axis=0, keepdims=True)
        p = jnp.exp2(s - m0).astype(bf16)
        pv = lax.dot_general(vm_ref[:, cols], p, (((0,), (0,)), ((), ())), preferred_element_type=f32)
        pv = pv[(h % 2) * HEAD_DIM:(h % 2 + 1) * HEAD_DIM]
        l0 = jnp.sum(p.astype(f32), axis=0, keepdims=True)
        acc_ref[h] = jnp.concatenate([pv, jnp.broadcast_to(l0, (ONES_ROWS, t))], axis=0)
        m_ref[h] = jnp.broadcast_to(m0, (SUBLANES, t))

    thr_b = thr_ref[0:1, :]

    def attn_body(kb, carry):
        r0 = pl.multiple_of(kb * t, t)
        bias = jnp.where(sc_ref[kb] >= thr_b, 0.0, -jnp.inf)
        alphas, m_news = [], []
        for h in range(ATTN_HEADS):
            cols = slice((h // 2) * LANES, (h // 2 + 1) * LANES)
            s = jnp.dot(ka_ref[pl.ds(r0, t), cols], q_head(h), preferred_element_type=f32) + bias
            s_ref[h] = s
            m_old = m_ref[h][0:1, :]
            m_new = jnp.maximum(m_old, jnp.max(s, axis=0, keepdims=True))
            m_ref[h] = jnp.broadcast_to(m_new, (SUBLANES, t))
            alphas.append(jnp.exp2(m_old - m_new))
            m_news.append(m_new)
        for h in range(ATTN_HEADS):
            p = jnp.exp2(s_ref[h] - m_news[h]).astype(bf16)
            pv = jnp.dot(vt_ref[kb, h * VT_ROWS:(h + 1) * VT_ROWS, :], p, preferred_element_type=f32)
            acc_ref[h] = alphas[h] * acc_ref[h] + pv
        return carry

    lax.fori_loop(0, nkb, attn_body, 0)

    outs = []
    for h in range(ATTN_HEADS):
        a = acc_ref[h]
        outs.append(a[:HEAD_DIM] / a[HEAD_DIM:HEAD_DIM + 1])
    o_ref[...] = jnp.concatenate(outs, axis=0).T


def _sparse_attention(qat, qit, wt, ki, ka, vt, km, vm, batch, seq):
    t = TILE
    n = ka.shape[0]
    nq = seq // t
    topk = min(TOPK_MAX, seq // 4)
    qcol = lambda b, i: (0, b * nq + i)
    brow = lambda b, i: (b, 0)
    const = lambda b, i: (0, 0)
    once = pl.Buffered(1)
    return pl.pallas_call(
        functools.partial(_attn_kernel, t=t, topk=topk),
        grid=(batch, nq),
        in_specs=[
            pl.BlockSpec((ATTN_WIDTH, t), qcol),
            pl.BlockSpec((ATTN_WIDTH, t), qcol),
            pl.BlockSpec((IDX_HEADS, t), qcol),
            pl.BlockSpec((seq, IDX_DIM), brow, pipeline_mode=once),
            pl.BlockSpec((seq, ATTN_WIDTH), brow, pipeline_mode=once),
            pl.BlockSpec((nq, ATTN_HEADS * VT_ROWS, t), lambda b, i: (b, 0, 0), pipeline_mode=once),
            pl.BlockSpec((N_META, ATTN_WIDTH), const),
            pl.BlockSpec((N_META, ATTN_WIDTH), const),
        ],
        out_specs=pl.BlockSpec((t, ATTN_WIDTH), lambda b, i: (b * nq + i, 0)),
        out_shape=jax.ShapeDtypeStruct((n, ATTN_WIDTH), f32),
        scratch_shapes=[
            pltpu.VMEM((nq, t, t), f32),
            pltpu.VMEM((ATTN_HEADS, t, t), f32),
            pltpu.VMEM((ATTN_HEADS, VT_ROWS, t), f32),
            pltpu.VMEM((ATTN_HEADS, SUBLANES, t), f32),
            pltpu.VMEM((SUBLANES, t), f32),
        ],
        compiler_params=pltpu.CompilerParams(
            dimension_semantics=("parallel", "arbitrary"), vmem_limit_bytes=VMEM_LIMIT),
        name="sparse_attn",
    )(qat, qit, wt, ki, ka, vt, km, vm)


def _hgrn_chunk(q, lf, k, v, st, tri, band_masks, blk_lower):
    b = jnp.dot(tri, lf, preferred_element_type=f32, precision=lax.Precision.HIGHEST)
    bl = b[CHUNK - 1:CHUNK, :]
    q_in = (q * jnp.exp(b)).astype(bf16)
    o = lax.dot_general(q_in, st.astype(bf16), (((1,), (1,)), ((), ())), preferred_element_type=f32)

    blocks = [jnp.zeros((SUB, CHUNK), f32)]
    for si in range(1, CHUNK // SUB):
        bref = b[si * SUB - 1:si * SUB, :]
        rows = slice(si * SUB, (si + 1) * SUB)
        q_s = (q[rows] * jnp.exp(b[rows] - bref)).astype(bf16)
        k_s = (k * jnp.exp(jnp.minimum(bref - b, 0.0))).astype(bf16)
        blocks.append(lax.dot_general(q_s, k_s, (((1,), (1,)), ((), ())), preferred_element_type=f32))
    a = jnp.where(blk_lower, jnp.concatenate(blocks, axis=0), 0.0)

    for dlt in range(SUB):
        if dlt == 0:
            w = q * k
        else:
            decay = jnp.exp(jnp.minimum(b - pltpu.roll(b, dlt, 0), 0.0))
            w = q * pltpu.roll(k, dlt, 0) * decay
        a = a + jnp.where(band_masks[dlt], jnp.sum(w, axis=1, keepdims=True), 0.0)

    o = o + jnp.dot(a.astype(bf16), v.astype(bf16), preferred_element_type=f32)
    k_end = (k * jnp.exp(bl - b)).astype(bf16)
    upd = lax.dot_general(v.astype(bf16), k_end, (((0,), (0,)), ((), ())), preferred_element_type=f32)
    return o, st * jnp.exp(bl) + upd


def _hgrn_kernel(q_ref, lf_ref, k_ref, v_ref, lfm_ref, km_ref, vm_ref, o_ref, st_ref, *, n_chunks):
    c = pl.program_id(1)
    r_i = lax.broadcasted_iota(i32, (CHUNK, CHUNK), 0)
    c_i = lax.broadcasted_iota(i32, (CHUNK, CHUNK), 1)
    tri = jnp.where(c_i <= r_i, 1.0, 0.0).astype(f32)
    sub_shift = SUB.bit_length() - 1
    blk_lower = (c_i >> sub_shift) < (r_i >> sub_shift)
    band_masks = [((r_i & (SUB - 1)) >= d) & (c_i == r_i - d) for d in range(SUB)]

    @pl.when(c == 0)
    def _():
        pad = jnp.zeros((CHUNK - N_META, HGRN_DIM), f32)
        for h in range(HGRN_HEADS):
            cols = slice(h * HGRN_DIM, (h + 1) * HGRN_DIM)
            lf = jnp.concatenate([pad, lfm_ref[:, cols]], axis=0)
            k = jnp.concatenate([pad, km_ref[:, cols]], axis=0)
            v = jnp.concatenate([pad, vm_ref[:, cols]], axis=0)
            _, st = _hgrn_chunk(jnp.zeros((CHUNK, HGRN_DIM), f32), lf, k, v,
                                jnp.zeros((HGRN_DIM, HGRN_DIM), f32), tri, band_masks, blk_lower)
            st_ref[h] = st

    def chunk_body(ci, carry):
        rows = pl.ds(pl.multiple_of(ci * CHUNK, CHUNK), CHUNK)
        for h in range(HGRN_HEADS):
            cols = slice(h * HGRN_DIM, (h + 1) * HGRN_DIM)
            o, st = _hgrn_chunk(q_ref[rows, cols], lf_ref[rows, cols], k_ref[rows, cols],
                                v_ref[rows, cols], st_ref[h], tri, band_masks, blk_lower)
            o_ref[rows, cols] = o
            st_ref[h] = st
        return carry

    lax.fori_loop(0, n_chunks, chunk_body, 0)


def _hgrn(hq, hlf, hk, hv, lfm, km, vm, batch, seq, tc):
    n = hq.shape[0]
    nc = seq // tc
    row = lambda b, c: (b * nc + c, 0)
    const = lambda b, c: (0, 0)
    spec = pl.BlockSpec((tc, HGRN_WIDTH), row)
    mspec = pl.BlockSpec((N_META, HGRN_WIDTH), const)
    return pl.pallas_call(
        functools.partial(_hgrn_kernel, n_chunks=tc // CHUNK),
        grid=(batch, nc),
        in_specs=[spec, spec, spec, spec, mspec, mspec, mspec],
        out_specs=spec,
        out_shape=jax.ShapeDtypeStruct((n, HGRN_WIDTH), f32),
        scratch_shapes=[pltpu.VMEM((HGRN_HEADS, HGRN_DIM, HGRN_DIM), f32)],
        compiler_params=pltpu.CompilerParams(
            dimension_semantics=("parallel", "arbitrary"), vmem_limit_bytes=VMEM_LIMIT),
        name="hgrn2",
    )(hq, hlf, hk, hv, lfm, km, vm)


def _outproj_kernel(oa_ref, ga_ref, oh_ref, hg_ref, x_ref, w_ref, na_ref, nh_ref, nf_ref, y_ref):
    oa = oa_ref[...]
    a = oa * lax.rsqrt(jnp.mean(oa * oa, axis=-1, keepdims=True) + EPS) * na_ref[...] * ga_ref[...]
    parts = []
    for h in range(HGRN_HEADS):
        cols = slice(h * HGRN_DIM, (h + 1) * HGRN_DIM)
        oh = oh_ref[:, cols]
        nrm = oh * lax.rsqrt(jnp.mean(oh * oh, axis=-1, keepdims=True) + EPS) * nh_ref[...]
        parts.append(nrm * hg_ref[:, cols])
    hh = jnp.concatenate(parts, axis=-1)
    mix = (jnp.dot(a.astype(bf16), w_ref[:ATTN_WIDTH, :], preferred_element_type=f32)
           + jnp.dot(hh.astype(bf16), w_ref[ATTN_WIDTH:, :], preferred_element_type=f32))
    hres = x_ref[...] + mix
    y_ref[...] = hres * lax.rsqrt(jnp.mean(hres * hres, axis=-1, keepdims=True) + EPS) * nf_ref[...]


def _outproj(oa, ga, oh, hg, x2, w, na, nh, nf, tm):
    n, d_model = x2.shape
    row = lambda i: (i, 0)
    const = lambda i: (0, 0)
    half = pl.BlockSpec((tm, 512), row)
    return pl.pallas_call(
        _outproj_kernel,
        grid=(n // tm,),
        in_specs=[half, half, half, half,
                  pl.BlockSpec((tm, d_model), row),
                  pl.BlockSpec(w.shape, const),
                  pl.BlockSpec((1, ATTN_WIDTH), const),
                  pl.BlockSpec((1, HGRN_DIM), const),
                  pl.BlockSpec((1, d_model), const)],
        out_specs=pl.BlockSpec((tm, d_model), row),
        out_shape=jax.ShapeDtypeStruct((n, d_model), f32),
        compiler_params=pltpu.CompilerParams(
            dimension_semantics=("parallel",), vmem_limit_bytes=VMEM_LIMIT),
        name="outproj",
    )(oa, ga, oh, hg, x2, w, na, nh, nf)


def _rope_tables(n_pos):
    inv = ROPE_THETA ** (-jnp.arange(ROPE_HALF, dtype=f32) * 2.0 / ROPE_DIMS)
    ang = jnp.arange(n_pos, dtype=f32)[:, None] * inv[None, :]
    cos, sin = jnp.cos(ang), jnp.sin(ang)
    rest = HEAD_DIM - ROPE_DIMS
    cos_h = jnp.concatenate([cos, cos, jnp.ones((n_pos, rest), f32)], axis=-1)
    sin_h = jnp.concatenate([-sin, sin, jnp.zeros((n_pos, rest), f32)], axis=-1)
    return jnp.tile(cos_h, (1, LANES // HEAD_DIM)), jnp.tile(sin_h, (1, LANES // HEAD_DIM))


def kernel(x, meta_tokens, mix_norm, w_in, idx_k_norm_g, idx_k_norm_b, attn_out_norm,
           hgrn_lb_logits, hgrn_out_norm, w_out, final_norm):
    batch, seq, d_model = x.shape
    assert seq % TILE == 0
    x2 = x.reshape(batch * seq, d_model)

    wl = w_in[0]
    a_end = 4 * ATTN_WIDTH
    iq_end = a_end + IDX_HEADS * IDX_DIM
    sm_end = iq_end + IDX_DIM + IDX_HEADS
    w_small = jnp.pad(wl[:, iq_end:sm_end], ((0, 0), (0, SMALL_W - (sm_end - iq_end))))
    w = jnp.concatenate([wl[:, :iq_end], wl[:, sm_end:], w_small], axis=-1).astype(bf16)

    cos_t, sin_t = _rope_tables(N_META + seq)
    g = mix_norm[0][None, :]
    lng = jnp.pad(idx_k_norm_g[0], (0, SMALL_W - IDX_DIM))[None, :]
    lnb = jnp.pad(idx_k_norm_b[0], (0, SMALL_W - IDX_DIM))[None, :]

    qat, ka, vt, ga, qit, ki, wt, hq, hlf, hk, hv, hg = _inproj(
        x2, g, w, cos_t[N_META:], sin_t[N_META:], lng, lnb, hgrn_lb_logits, seq // TILE)
    ka_m, va_m, hlf_m, hk_m, hv_m = _meta_proj(
        meta_tokens, g, w, cos_t[:N_META], sin_t[:N_META], hgrn_lb_logits)

    oa = _sparse_attention(qat, qit, wt, ki, ka, vt, ka_m, va_m, batch, seq)
    oh = _hgrn(hq, hlf, hk, hv, hlf_m, hk_m, hv_m, batch, seq, tc=512)
    y = _outproj(oa, ga, oh, hg, x2, w_out[0].astype(bf16), attn_out_norm[0][None, :],
                 hgrn_out_norm[0][None, :], final_norm[None, :], tm=256)
    return y.reshape(batch, seq, d_model)
```

```python
import functools
import math

import jax
import jax.numpy as jnp
from jax import lax
from jax.experimental import pallas as pl
from jax.experimental.pallas import tpu as pltpu

N_META = 16
ATTN_HEADS = 8
HEAD_DIM = 64
ATTN_WIDTH = ATTN_HEADS * HEAD_DIM
IDX_HEADS = 8
IDX_DIM = 64
IDX_SCALE = (IDX_HEADS ** -0.5) * (IDX_DIM ** -0.5)
ATTN_SCALE = HEAD_DIM ** -0.5
LOG2E = math.log2(math.e)
TOPK_MAX = 256
HGRN_HEADS = 4
HGRN_DIM = 128
HGRN_WIDTH = HGRN_HEADS * HGRN_DIM
CHUNK = 64
SUB = 16
ROPE_THETA = 500000.0
ROPE_DIMS = HEAD_DIM // 4
ROPE_HALF = ROPE_DIMS // 2
EPS = 1e-6

LANES = 128
SUBLANES = 8
VMEM_LIMIT = 56 * 1024 * 1024

TILE = 256
N_WIDE = 9
SMALL_W = 128
ONES_ROWS = 16
VT_ROWS = HEAD_DIM + ONES_ROWS
INT_MAX = 2 ** 31 - 1
KEY_NEG_INF = -(2 ** 23) ^ INT_MAX
KEY_POS_INF = 0x7F800000
FLT_LOWEST = -3.4028234663852886e38
MAX_BISECT = 36
BISECT_PER_CHECK = 4

f32 = jnp.float32
bf16 = jnp.bfloat16
i32 = jnp.int32


def _sigmoid(v):
    return 1.0 / (1.0 + jnp.exp(-v))


def _normed_rows(x_ref, g_ref):
    x = x_ref[...]
    ms = jnp.mean(x * x, axis=-1, keepdims=True)
    return ((x * lax.rsqrt(ms + EPS)) * g_ref[...]).astype(bf16)


def _rope_fn(cos1, sin1, tm):
    cos = jnp.concatenate([cos1] * 4, axis=-1)
    sin = jnp.concatenate([sin1] * 4, axis=-1)
    lane = lax.broadcasted_iota(i32, (tm, ATTN_WIDTH), 1)
    low = (lane & (HEAD_DIM - 1)) < ROPE_HALF

    def rope(v):
        partner = jnp.where(low, pltpu.roll(v, ATTN_WIDTH - ROPE_HALF, 1),
                            pltpu.roll(v, ROPE_HALF, 1))
        return v * cos + partner * sin
    return rope


def _hgrn_gates(lbl_ref, hf):
    lg = lbl_ref[...]
    e = jnp.exp(lg - jnp.max(lg, axis=0, keepdims=True))
    lb = e[0:1, :] / jnp.sum(e, axis=0, keepdims=True)
    f = lb + (1.0 - lb) * _sigmoid(hf)
    return jnp.log(f), 1.0 - f


def _inproj_kernel(x_ref, g_ref, w_ref, cos_ref, sin_ref, lng_ref, lnb_ref, lbl_ref,
                   qat_ref, ka_ref, vt_ref, ga_ref, qit_ref, ki_ref, wt_ref,
                   hq_ref, hlf_ref, hk_ref, hv_ref, hg_ref):
    xb = _normed_rows(x_ref, g_ref)
    tm = xb.shape[0]
    cos1 = cos_ref[...]
    sin1 = sin_ref[...]
    rope = _rope_fn(cos1, sin1, tm)

    def proj(g):
        return jnp.dot(xb, w_ref[:, g * 512:(g + 1) * 512], preferred_element_type=f32)

    qat_ref[...] = (rope(proj(0)) * (ATTN_SCALE * LOG2E)).T.astype(bf16)
    ka_ref[...] = rope(proj(1)).astype(bf16)
    vt = proj(2).T
    ones = jnp.ones((ONES_ROWS, tm), f32)
    parts = []
    for h in range(ATTN_HEADS):
        parts += [vt[h * HEAD_DIM:(h + 1) * HEAD_DIM], ones]
    vt_ref[0] = jnp.concatenate(parts, axis=0).astype(bf16)
    ag = proj(3)
    ga_ref[...] = ag * _sigmoid(ag)
    qit_ref[...] = rope(proj(4)).T.astype(bf16)

    hq = proj(5)
    hq_ref[...] = hq * _sigmoid(hq)
    hlf_ref[...], hk_ref[...] = _hgrn_gates(lbl_ref, proj(6))
    hv_ref[...] = proj(7)
    hg = proj(8)
    hg_ref[...] = hg * _sigmoid(hg)

    sm = jnp.dot(xb, w_ref[:, N_WIDE * 512:N_WIDE * 512 + SMALL_W], preferred_element_type=f32)
    lane1 = lax.broadcasted_iota(i32, (tm, SMALL_W), 1)
    is_key = lane1 < IDX_DIM
    mu = jnp.sum(jnp.where(is_key, sm, 0.0), axis=-1, keepdims=True) / IDX_DIM
    d = sm - mu
    var = jnp.sum(jnp.where(is_key, d * d, 0.0), axis=-1, keepdims=True) / IDX_DIM
    y = d * lax.rsqrt(var + EPS) * lng_ref[...] + lnb_ref[...]
    low1 = (lane1 & (IDX_DIM - 1)) < ROPE_HALF
    partner = jnp.where(low1, pltpu.roll(y, SMALL_W - ROPE_HALF, 1), pltpu.roll(y, ROPE_HALF, 1))
    y = y * cos1 + partner * sin1
    ki_ref[...] = y[:, :IDX_DIM].astype(bf16)
    wt_ref[...] = sm.T[IDX_DIM:IDX_DIM + IDX_HEADS] * IDX_SCALE


def _inproj(x2, g, w, cos_t, sin_t, lng, lnb, lbl, n_pos_blocks):
    n, d_model = x2.shape
    tm = TILE
    row = lambda i: (i, 0)
    col = lambda i: (0, i)
    const = lambda i: (0, 0)
    pos = lambda i: (i % n_pos_blocks, 0)
    wide_bf = jax.ShapeDtypeStruct((n, 512), bf16)
    wide_f = jax.ShapeDtypeStruct((n, 512), f32)
    wide_t = jax.ShapeDtypeStruct((512, n), bf16)
    out_shape = (wide_t, wide_bf,
                 jax.ShapeDtypeStruct((n // tm, ATTN_HEADS * VT_ROWS, tm), bf16),
                 wide_f, wide_t,
                 jax.ShapeDtypeStruct((n, IDX_DIM), bf16),
                 jax.ShapeDtypeStruct((IDX_HEADS, n), f32),
                 wide_f, wide_f, wide_f, wide_f, wide_f)
    wide_spec = pl.BlockSpec((tm, 512), row)
    t_spec = pl.BlockSpec((512, tm), col)
    out_specs = (t_spec, wide_spec,
                 pl.BlockSpec((1, ATTN_HEADS * VT_ROWS, tm), lambda i: (i, 0, 0)),
                 wide_spec, t_spec,
                 pl.BlockSpec((tm, IDX_DIM), row),
                 pl.BlockSpec((IDX_HEADS, tm), col)) + (wide_spec,) * 5
    return pl.pallas_call(
        _inproj_kernel,
        grid=(n // tm,),
        in_specs=[
            pl.BlockSpec((tm, d_model), row),
            pl.BlockSpec((1, d_model), const),
            pl.BlockSpec(w.shape, const),
            pl.BlockSpec((tm, LANES), pos),
            pl.BlockSpec((tm, LANES), pos),
            pl.BlockSpec((1, SMALL_W), const),
            pl.BlockSpec((1, SMALL_W), const),
            pl.BlockSpec(lbl.shape, const),
        ],
        out_specs=out_specs,
        out_shape=out_shape,
        compiler_params=pltpu.CompilerParams(
            dimension_semantics=("parallel",), vmem_limit_bytes=VMEM_LIMIT),
        name="inproj",
    )(x2, g, w, cos_t, sin_t, lng, lnb, lbl)


def _meta_proj_kernel(x_ref, g_ref, w_ref, cos_ref, sin_ref, lbl_ref,
                      ka_ref, va_ref, hlf_ref, hk_ref, hv_ref):
    xb = _normed_rows(x_ref, g_ref)
    rope = _rope_fn(cos_ref[...], sin_ref[...], xb.shape[0])

    def proj(g):
        return jnp.dot(xb, w_ref[:, g * 512:(g + 1) * 512], preferred_element_type=f32)

    ka_ref[...] = rope(proj(1)).astype(bf16)
    va_ref[...] = proj(2).astype(bf16)
    hlf_ref[...], hk_ref[...] = _hgrn_gates(lbl_ref, proj(6))
    hv_ref[...] = proj(7)


def _meta_proj(meta, g, w, cos_t, sin_t, lbl):
    n, d_model = meta.shape
    full = lambda a: pl.BlockSpec(a.shape, lambda i: (0,) * a.ndim)
    wide = pl.BlockSpec((n, 512), lambda i: (0, 0))
    return pl.pallas_call(
        _meta_proj_kernel,
        grid=(1,),
        in_specs=[full(meta), full(g), full(w), full(cos_t), full(sin_t), full(lbl)],
        out_specs=(wide,) * 5,
        out_shape=(jax.ShapeDtypeStruct((n, 512), bf16),) * 2
        + (jax.ShapeDtypeStruct((n, 512), f32),) * 3,
        compiler_params=pltpu.CompilerParams(vmem_limit_bytes=VMEM_LIMIT),
        name="meta_proj",
    )(meta, g, w, cos_t, sin_t, lbl)


def _unkey(k):
    return lax.bitcast_convert_type(k ^ ((k >> 31) & INT_MAX), f32)


def _attn_kernel(qat_ref, qit_ref, wt_ref, ki_ref, ka_ref, vt_ref, km_ref, vm_ref, o_ref,
                 sc_ref, s_ref, acc_ref, m_ref, thr_ref, *, t, topk):
    i = pl.program_id(1)
    nkb = i + 1
    col_bits = max(1, (sc_ref.shape[0] * t - 1).bit_length())
    key_l = lax.broadcasted_iota(i32, (t, t), 0)
    qry_l = lax.broadcasted_iota(i32, (t, t), 1)

    def score_body(kb, carry):
        kik = ki_ref[pl.ds(pl.multiple_of(kb * t, t), t), :]
        acc = jnp.zeros((t, t), f32)
        for h in range(IDX_HEADS):
            s = jnp.dot(kik, qit_ref[h * IDX_DIM:(h + 1) * IDX_DIM, :], preferred_element_type=f32)
            acc = acc + jnp.maximum(s, 0.0) * wt_ref[h:h + 1, :]
        sc_ref[kb] = acc
        return carry

    lax.fori_loop(0, nkb, score_body, 0)
    sc_ref[i] = jnp.where(key_l <= qry_l, sc_ref[i], -jnp.inf)

    sub_l = lax.broadcasted_iota(i32, (SUBLANES, t), 0)

    def count(pred):
        def body(kb, accs):
            accs = list(accs)
            for r in range(t // SUBLANES):
                v = sc_ref[kb, r * SUBLANES:(r + 1) * SUBLANES, :]
                accs[r % 4] = accs[r % 4] + jnp.where(pred(v, kb * t + r * SUBLANES + sub_l), 1, 0)
            return tuple(accs)
        z = jnp.zeros((SUBLANES, t), i32)
        a = lax.fori_loop(0, nkb, body, (z, z, z, z))
        return jnp.sum((a[0] + a[1]) + (a[2] + a[3]), axis=0, keepdims=True)

    def cond(carry):
        it, lo, hi, c_lo, c_hi = carry
        open_ = jnp.where((c_lo != topk) & (hi != lo + 1), 1.0, 0.0)
        return (it < MAX_BISECT) & (jnp.max(open_) > 0.0)

    def step(carry):
        it, lo, hi, c_lo, c_hi = carry
        for _ in range(BISECT_PER_CHECK):
            mid = lo + lax.shift_right_logical(hi - lo, 1)
            cb = jnp.broadcast_to(_unkey(mid), (SUBLANES, t))
            c = count(lambda v, idx: v >= cb)
            ok = c >= topk
            lo, hi = jnp.where(ok, mid, lo), jnp.where(ok, hi, mid)
            c_lo, c_hi = jnp.where(ok, c, c_lo), jnp.where(ok, c_hi, c)
        return it + BISECT_PER_CHECK, lo, hi, c_lo, c_hi

    row = lambda val: jnp.full((1, t), val, i32)
    _, lo, hi, c_lo, c_hi = lax.while_loop(
        cond, step, (jnp.int32(0), row(KEY_NEG_INF), row(KEY_POS_INF), row(0) + nkb * t, row(0)))
    thr = _unkey(lo)
    thr_ref[...] = jnp.broadcast_to(jnp.maximum(thr, FLT_LOWEST), (SUBLANES, t))

    @pl.when(jnp.max(c_lo) > topk)
    def _():
        tb = jnp.broadcast_to(thr, (SUBLANES, t))
        need = topk - c_hi

        def idx_body(j, jcol):
            cand = jcol | (1 << (col_bits - 1 - j))
            cb = jnp.broadcast_to(cand, (SUBLANES, t))
            c = count(lambda v, idx: (v == tb) & (idx < cb))
            return jnp.where(c < need, cand, jcol)

        jcol = lax.fori_loop(0, col_bits, idx_body, jnp.zeros((1, t), i32))
        jcol = jnp.where(c_lo > topk, jcol, INT_MAX)

        def drop_body(kb, carry):
            v = sc_ref[kb]
            sc_ref[kb] = jnp.where((v == thr) & (kb * t + key_l > jcol), -jnp.inf, v)
            return carry

        lax.fori_loop(0, nkb, drop_body, 0)


    row_h = lax.broadcasted_iota(i32, (2 * HEAD_DIM, t), 0)
    first = row_h < HEAD_DIM

    def q_head(h):
        qp = qat_ref[(h // 2) * LANES:(h // 2 + 1) * LANES, :]
        return jnp.where(first if h % 2 == 0 else ~first, qp, jnp.zeros_like(qp))

    for h in range(ATTN_HEADS):
        cols = slice((h // 2) * LANES, (h // 2 + 1) * LANES)
        s = jnp.dot(km_ref[:, cols], q_head(h), preferred_element_type=f32)
        m0 = jnp.max(s, axis=0, keepdims=True)
        p = jnp.exp2(s - m0).astype(bf16)
        pv = lax.dot_general(vm_ref[:, cols], p, (((0,), (0,)), ((), ())), preferred_element_type=f32)
        pv = pv[(h % 2) * HEAD_DIM:(h % 2 + 1) * HEAD_DIM]
        l0 = jnp.sum(p.astype(f32), axis=0, keepdims=True)
        acc_ref[h] = jnp.concatenate([pv, jnp.broadcast_to(l0, (ONES_ROWS, t))], axis=0)
        m_ref[h] = jnp.broadcast_to(m0, (SUBLANES, t))

    thr_b = thr_ref[0:1, :]

    def attn_body(kb, carry):
        r0 = pl.multiple_of(kb * t, t)
        bias = jnp.where(sc_ref[kb] >= thr_b, 0.0, -jnp.inf)
        alphas, m_news = [], []
        for h in range(ATTN_HEADS):
            cols = slice((h // 2) * LANES, (h // 2 + 1) * LANES)
            s = jnp.dot(ka_ref[pl.ds(r0, t), cols], q_head(h), preferred_element_type=f32) + bias
            s_ref[h] = s
            m_old = m_ref[h][0:1, :]
            m_new = jnp.maximum(m_old, jnp.max(s, axis=0, keepdims=True))
            m_ref[h] = jnp.broadcast_to(m_new, (SUBLANES, t))
            alphas.append(jnp.exp2(m_old - m_new))
            m_news.append(m_new)
        for h in range(ATTN_HEADS):
            p = jnp.exp2(s_ref[h] - m_news[h]).astype(bf16)
            pv = jnp.dot(vt_ref[kb, h * VT_ROWS:(h + 1) * VT_ROWS, :], p, preferred_element_type=f32)
            acc_ref[h] = alphas[h] * acc_ref[h] + pv
        return carry

    lax.fori_loop(0, nkb, attn_body, 0)

    outs = []
    for h in range(ATTN_HEADS):
        a = acc_ref[h]
        outs.append(a[:HEAD_DIM] / a[HEAD_DIM:HEAD_DIM + 1])
    o_ref[...] = jnp.concatenate(outs, axis=0).T


def _sparse_attention(qat, qit, wt, ki, ka, vt, km, vm, batch, seq):
    t = TILE
    n = ka.shape[0]
    nq = seq // t
    topk = min(TOPK_MAX, seq // 4)
    qcol = lambda b, i: (0, b * nq + i)
    brow = lambda b, i: (b, 0)
    const = lambda b, i: (0, 0)
    once = pl.Buffered(1)
    return pl.pallas_call(
        functools.partial(_attn_kernel, t=t, topk=topk),
        grid=(batch, nq),
        in_specs=[
            pl.BlockSpec((ATTN_WIDTH, t), qcol),
            pl.BlockSpec((ATTN_WIDTH, t), qcol),
            pl.BlockSpec((IDX_HEADS, t), qcol),
            pl.BlockSpec((seq, IDX_DIM), brow, pipeline_mode=once),
            pl.BlockSpec((seq, ATTN_WIDTH), brow, pipeline_mode=once),
            pl.BlockSpec((nq, ATTN_HEADS * VT_ROWS, t), lambda b, i: (b, 0, 0), pipeline_mode=once),
            pl.BlockSpec((N_META, ATTN_WIDTH), const),
            pl.BlockSpec((N_META, ATTN_WIDTH), const),
        ],
        out_specs=pl.BlockSpec((t, ATTN_WIDTH), lambda b, i: (b * nq + i, 0)),
        out_shape=jax.ShapeDtypeStruct((n, ATTN_WIDTH), f32),
        scratch_shapes=[
            pltpu.VMEM((nq, t, t), f32),
            pltpu.VMEM((ATTN_HEADS, t, t), f32),
            pltpu.VMEM((ATTN_HEADS, VT_ROWS, t), f32),
            pltpu.VMEM((ATTN_HEADS, SUBLANES, t), f32),
            pltpu.VMEM((SUBLANES, t), f32),
        ],
        compiler_params=pltpu.CompilerParams(
            dimension_semantics=("parallel", "arbitrary"), vmem_limit_bytes=VMEM_LIMIT),
        name="sparse_attn",
    )(qat, qit, wt, ki, ka, vt, km, vm)


def _hgrn_chunk(q, lf, k, v, st, tri, band_masks, blk_lower):
    b = jnp.dot(tri, lf, preferred_element_type=f32, precision=lax.Precision.HIGHEST)
    bl = b[CHUNK - 1:CHUNK, :]
    q_in = (q * jnp.exp(b)).astype(bf16)
    o = lax.dot_general(q_in, st.astype(bf16), (((1,), (1,)), ((), ())), preferred_element_type=f32)

    blocks = [jnp.zeros((SUB, CHUNK), f32)]
    for si in range(1, CHUNK // SUB):
        bref = b[si * SUB - 1:si * SUB, :]
        rows = slice(si * SUB, (si + 1) * SUB)
        q_s = (q[rows] * jnp.exp(b[rows] - bref)).astype(bf16)
        k_s = (k * jnp.exp(jnp.minimum(bref - b, 0.0))).astype(bf16)
        blocks.append(lax.dot_general(q_s, k_s, (((1,), (1,)), ((), ())), preferred_element_type=f32))
    a = jnp.where(blk_lower, jnp.concatenate(blocks, axis=0), 0.0)

    for dlt in range(SUB):
        if dlt == 0:
            w = q * k
        else:
            decay = jnp.exp(jnp.minimum(b - pltpu.roll(b, dlt, 0), 0.0))
            w = q * pltpu.roll(k, dlt, 0) * decay
        a = a + jnp.where(band_masks[dlt], jnp.sum(w, axis=1, keepdims=True), 0.0)

    o = o + jnp.dot(a.astype(bf16), v.astype(bf16), preferred_element_type=f32)
    k_end = (k * jnp.exp(bl - b)).astype(bf16)
    upd = lax.dot_general(v.astype(bf16), k_end, (((0,), (0,)), ((), ())), preferred_element_type=f32)
    return o, st * jnp.exp(bl) + upd


def _hgrn_kernel(q_ref, lf_ref, k_ref, v_ref, lfm_ref, km_ref, vm_ref, o_ref, st_ref, *, n_chunks):
    c = pl.program_id(1)
    r_i = lax.broadcasted_iota(i32, (CHUNK, CHUNK), 0)
    c_i = lax.broadcasted_iota(i32, (CHUNK, CHUNK), 1)
    tri = jnp.where(c_i <= r_i, 1.0, 0.0).astype(f32)
    sub_shift = SUB.bit_length() - 1
    blk_lower = (c_i >> sub_shift) < (r_i >> sub_shift)
    band_masks = [((r_i & (SUB - 1)) >= d) & (c_i == r_i - d) for d in range(SUB)]

    @pl.when(c == 0)
    def _():
        pad = jnp.zeros((CHUNK - N_META, HGRN_DIM), f32)
        for h in range(HGRN_HEADS):
            cols = slice(h * HGRN_DIM, (h + 1) * HGRN_DIM)
            lf = jnp.concatenate([pad, lfm_ref[:, cols]], axis=0)
            k = jnp.concatenate([pad, km_ref[:, cols]], axis=0)
            v = jnp.concatenate([pad, vm_ref[:, cols]], axis=0)
            _, st = _hgrn_chunk(jnp.zeros((CHUNK, HGRN_DIM), f32), lf, k, v,
                                jnp.zeros((HGRN_DIM, HGRN_DIM), f32), tri, band_masks, blk_lower)
            st_ref[h] = st

    def chunk_body(ci, carry):
        rows = pl.ds(pl.multiple_of(ci * CHUNK, CHUNK), CHUNK)
        for h in range(HGRN_HEADS):
            cols = slice(h * HGRN_DIM, (h + 1) * HGRN_DIM)
            o, st = _hgrn_chunk(q_ref[rows, cols], lf_ref[rows, cols], k_ref[rows, cols],
                                v_ref[rows, cols], st_ref[h], tri, band_masks, blk_lower)
            o_ref[rows, cols] = o
            st_ref[h] = st
        return carry

    lax.fori_loop(0, n_chunks, chunk_body, 0)


def _hgrn(hq, hlf, hk, hv, lfm, km, vm, batch, seq, tc):
    n = hq.shape[0]
    nc = seq // tc
    row = lambda b, c: (b * nc + c, 0)
    const = lambda b, c: (0, 0)
    spec = pl.BlockSpec((tc, HGRN_WIDTH), row)
    mspec = pl.BlockSpec((N_META, HGRN_WIDTH), const)
    return pl.pallas_call(
        functools.partial(_hgrn_kernel, n_chunks=tc // CHUNK),
        grid=(batch, nc),
        in_specs=[spec, spec, spec, spec, mspec, mspec, mspec],
        out_specs=spec,
        out_shape=jax.ShapeDtypeStruct((n, HGRN_WIDTH), f32),
        scratch_shapes=[pltpu.VMEM((HGRN_HEADS, HGRN_DIM, HGRN_DIM), f32)],
        compiler_params=pltpu.CompilerParams(
            dimension_semantics=("parallel", "arbitrary"), vmem_limit_bytes=VMEM_LIMIT),
        name="hgrn2",
    )(hq, hlf, hk, hv, lfm, km, vm)


def _outproj_kernel(oa_ref, ga_ref, oh_ref, hg_ref, x_ref, w_ref, na_ref, nh_ref, nf_ref, y_ref):
    oa = oa_ref[...]
    a = oa * lax.rsqrt(jnp.mean(oa * oa, axis=-1, keepdims=True) + EPS) * na_ref[...] * ga_ref[...]
    parts = []
    for h in range(HGRN_HEADS):
        cols = slice(h * HGRN_DIM, (h + 1) * HGRN_DIM)
        oh = oh_ref[:, cols]
        nrm = oh * lax.rsqrt(jnp.mean(oh * oh, axis=-1, keepdims=True) + EPS) * nh_ref[...]
        parts.append(nrm * hg_ref[:, cols])
    hh = jnp.concatenate(parts, axis=-1)
    mix = (jnp.dot(a.astype(bf16), w_ref[:ATTN_WIDTH, :], preferred_element_type=f32)
           + jnp.dot(hh.astype(bf16), w_ref[ATTN_WIDTH:, :], preferred_element_type=f32))
    hres = x_ref[...] + mix
    y_ref[...] = hres * lax.rsqrt(jnp.mean(hres * hres, axis=-1, keepdims=True) + EPS) * nf_ref[...]


def _outproj(oa, ga, oh, hg, x2, w, na, nh, nf, tm):
    n, d_model = x2.shape
    row = lambda i: (i, 0)
    const = lambda i: (0, 0)
    half = pl.BlockSpec((tm, 512), row)
    return pl.pallas_call(
        _outproj_kernel,
        grid=(n // tm,),
        in_specs=[half, half, half, half,
                  pl.BlockSpec((tm, d_model), row),
                  pl.BlockSpec(w.shape, const),
                  pl.BlockSpec((1, ATTN_WIDTH), const),
                  pl.BlockSpec((1, HGRN_DIM), const),
                  pl.BlockSpec((1, d_model), const)],
        out_specs=pl.BlockSpec((tm, d_model), row),
        out_shape=jax.ShapeDtypeStruct((n, d_model), f32),
        compiler_params=pltpu.CompilerParams(
            dimension_semantics=("parallel",), vmem_limit_bytes=VMEM_LIMIT),
        name="outproj",
    )(oa, ga, oh, hg, x2, w, na, nh, nf)


def _rope_tables(n_pos):
    inv = ROPE_THETA ** (-jnp.arange(ROPE_HALF, dtype=f32) * 2.0 / ROPE_DIMS)
    ang = jnp.arange(n_pos, dtype=f32)[:, None] * inv[None, :]
    cos, sin = jnp.cos(ang), jnp.sin(ang)
    rest = HEAD_DIM - ROPE_DIMS
    cos_h = jnp.concatenate([cos, cos, jnp.ones((n_pos, rest), f32)], axis=-1)
    sin_h = jnp.concatenate([-sin, sin, jnp.zeros((n_pos, rest), f32)], axis=-1)
    return jnp.tile(cos_h, (1, LANES // HEAD_DIM)), jnp.tile(sin_h, (1, LANES // HEAD_DIM))


def kernel(x, meta_tokens, mix_norm, w_in, idx_k_norm_g, idx_k_norm_b, attn_out_norm,
           hgrn_lb_logits, hgrn_out_norm, w_out, final_norm):
    batch, seq, d_model = x.shape
    assert seq % TILE == 0
    x2 = x.reshape(batch * seq, d_model)

    wl = w_in[0]
    a_end = 4 * ATTN_WIDTH
    iq_end = a_end + IDX_HEADS * IDX_DIM
    sm_end = iq_end + IDX_DIM + IDX_HEADS
    w_small = jnp.pad(wl[:, iq_end:sm_end], ((0, 0), (0, SMALL_W - (sm_end - iq_end))))
    w = jnp.concatenate([wl[:, :iq_end], wl[:, sm_end:], w_small], axis=-1).astype(bf16)

    cos_t, sin_t = _rope_tables(N_META + seq)
    g = mix_norm[0][None, :]
    lng = jnp.pad(idx_k_norm_g[0], (0, SMALL_W - IDX_DIM))[None, :]
    lnb = jnp.pad(idx_k_norm_b[0], (0, SMALL_W - IDX_DIM))[None, :]

    qat, ka, vt, ga, qit, ki, wt, hq, hlf, hk, hv, hg = _inproj(
        x2, g, w, cos_t[N_META:], sin_t[N_META:], lng, lnb, hgrn_lb_logits, seq // TILE)
    ka_m, va_m, hlf_m, hk_m, hv_m = _meta_proj(
        meta_tokens, g, w, cos_t[:N_META], sin_t[:N_META], hgrn_lb_logits)

    oa = _sparse_attention(qat, qit, wt, ki, ka, vt, ka_m, va_m, batch, seq)
    oh = _hgrn(hq, hlf, hk, hv, hlf_m, hk_m, hv_m, batch, seq, tc=512)
    y = _outproj(oa, ga, oh, hg, x2, w_out[0].astype(bf16), attn_out_norm[0][None, :],
                 hgrn_out_norm[0][None, :], final_norm[None, :], tm=256)
    return y.reshape(batch, seq, d_model)
```

```python
import functools
import math

import jax
import jax.numpy as jnp
from jax import lax
from jax.experimental import pallas as pl
from jax.experimental.pallas import tpu as pltpu

N_META = 16
ATTN_HEADS = 8
HEAD_DIM = 64
ATTN_WIDTH = ATTN_HEADS * HEAD_DIM
IDX_HEADS = 8
IDX_DIM = 64
IDX_SCALE = (IDX_HEADS ** -0.5) * (IDX_DIM ** -0.5)
ATTN_SCALE = HEAD_DIM ** -0.5
LOG2E = math.log2(math.e)
TOPK_MAX = 256
HGRN_HEADS = 4
HGRN_DIM = 128
HGRN_WIDTH = HGRN_HEADS * HGRN_DIM
CHUNK = 64
SUB = 16
ROPE_THETA = 500000.0
ROPE_DIMS = HEAD_DIM // 4
ROPE_HALF = ROPE_DIMS // 2
EPS = 1e-6

LANES = 128
SUBLANES = 8
VMEM_LIMIT = 56 * 1024 * 1024

TILE = 256
N_WIDE = 9
SMALL_W = 128
ONES_ROWS = 16
VT_ROWS = HEAD_DIM + ONES_ROWS
INT_MAX = 2 ** 31 - 1
INT_MIN = -(2 ** 31)
KEY_POS_INF = 0x7F800000
KEY_NEG_INF = -KEY_POS_INF
PACK16 = 16
FLT_LOWEST = -3.4028234663852886e38
MAX_BISECT = 20
BISECT_PER_CHECK = 4

f32 = jnp.float32
bf16 = jnp.bfloat16
i32 = jnp.int32
i16 = jnp.int16
coarse = jnp.bfloat16


def _sigmoid(v):
    return 1.0 / (1.0 + jnp.exp(-v))


def _normed_rows(x_ref, g_ref):
    x = x_ref[...]
    ms = jnp.mean(x * x, axis=-1, keepdims=True)
    return ((x * lax.rsqrt(ms + EPS)) * g_ref[...]).astype(bf16)


def _rope_fn(cos1, sin1, tm):
    cos = jnp.concatenate([cos1] * 4, axis=-1)
    sin = jnp.concatenate([sin1] * 4, axis=-1)
    lane = lax.broadcasted_iota(i32, (tm, ATTN_WIDTH), 1)
    low = (lane & (HEAD_DIM - 1)) < ROPE_HALF

    def rope(v):
        partner = jnp.where(low, pltpu.roll(v, ATTN_WIDTH - ROPE_HALF, 1),
                            pltpu.roll(v, ROPE_HALF, 1))
        return v * cos + partner * sin
    return rope


def _hgrn_gates(lbl_ref, hf):
    lg = lbl_ref[...]
    e = jnp.exp(lg - jnp.max(lg, axis=0, keepdims=True))
    lb = e[0:1, :] / jnp.sum(e, axis=0, keepdims=True)
    f = lb + (1.0 - lb) * _sigmoid(hf)
    return jnp.log(f), 1.0 - f


def _inproj_kernel(x_ref, g_ref, w_ref, cos_ref, sin_ref, lng_ref, lnb_ref, lbl_ref,
                   qat_ref, ka_ref, vt_ref, ga_ref, qit_ref, ki_ref, wt_ref,
                   hq_ref, hlf_ref, hk_ref, hv_ref, hg_ref):
    xb = _normed_rows(x_ref, g_ref)
    tm = xb.shape[0]
    cos1 = cos_ref[...]
    sin1 = sin_ref[...]
    rope = _rope_fn(cos1, sin1, tm)

    def proj(g):
        return jnp.dot(xb, w_ref[:, g * 512:(g + 1) * 512], preferred_element_type=f32)

    qat_ref[...] = (rope(proj(0)) * (ATTN_SCALE * LOG2E)).T.astype(bf16)
    ka_ref[...] = rope(proj(1)).astype(bf16)
    vt = proj(2).T
    ones = jnp.ones((ONES_ROWS, tm), f32)
    parts = []
    for h in range(ATTN_HEADS):
        parts += [vt[h * HEAD_DIM:(h + 1) * HEAD_DIM], ones]
    vt_ref[0] = jnp.concatenate(parts, axis=0).astype(bf16)
    ag = proj(3)
    ga_ref[...] = (ag * _sigmoid(ag)).astype(bf16)
    qit_ref[...] = rope(proj(4)).T.astype(bf16)

    hq = proj(5)
    hq_ref[...] = hq * _sigmoid(hq)
    hlf_ref[...], hk_ref[...] = _hgrn_gates(lbl_ref, proj(6))
    hv_ref[...] = proj(7)
    hg = proj(8)
    hg_ref[...] = (hg * _sigmoid(hg)).astype(bf16)

    sm = jnp.dot(xb, w_ref[:, N_WIDE * 512:N_WIDE * 512 + SMALL_W], preferred_element_type=f32)
    lane1 = lax.broadcasted_iota(i32, (tm, SMALL_W), 1)
    is_key = lane1 < IDX_DIM
    mu = jnp.sum(jnp.where(is_key, sm, 0.0), axis=-1, keepdims=True) / IDX_DIM
    d = sm - mu
    var = jnp.sum(jnp.where(is_key, d * d, 0.0), axis=-1, keepdims=True) / IDX_DIM
    y = d * lax.rsqrt(var + EPS) * lng_ref[...] + lnb_ref[...]
    low1 = (lane1 & (IDX_DIM - 1)) < ROPE_HALF
    partner = jnp.where(low1, pltpu.roll(y, SMALL_W - ROPE_HALF, 1), pltpu.roll(y, ROPE_HALF, 1))
    y = y * cos1 + partner * sin1
    ki_ref[...] = y[:, :IDX_DIM].astype(bf16)
    wt_ref[...] = sm.T[IDX_DIM:IDX_DIM + IDX_HEADS] * IDX_SCALE


def _inproj(x2, g, w, cos_t, sin_t, lng, lnb, lbl, n_pos_blocks):
    n, d_model = x2.shape
    tm = TILE
    row = lambda i: (i, 0)
    col = lambda i: (0, i)
    const = lambda i: (0, 0)
    pos = lambda i: (i % n_pos_blocks, 0)
    wide_bf = jax.ShapeDtypeStruct((n, 512), bf16)
    wide_f = jax.ShapeDtypeStruct((n, 512), f32)
    wide_t = jax.ShapeDtypeStruct((512, n), bf16)
    out_shape = (wide_t, wide_bf,
                 jax.ShapeDtypeStruct((n // tm, ATTN_HEADS * VT_ROWS, tm), bf16),
                 wide_bf, wide_t,
                 jax.ShapeDtypeStruct((n, IDX_DIM), bf16),
                 jax.ShapeDtypeStruct((IDX_HEADS, n), f32),
                 wide_f, wide_f, wide_f, wide_f, wide_bf)
    wide_spec = pl.BlockSpec((tm, 512), row)
    t_spec = pl.BlockSpec((512, tm), col)
    out_specs = (t_spec, wide_spec,
                 pl.BlockSpec((1, ATTN_HEADS * VT_ROWS, tm), lambda i: (i, 0, 0)),
                 wide_spec, t_spec,
                 pl.BlockSpec((tm, IDX_DIM), row),
                 pl.BlockSpec((IDX_HEADS, tm), col)) + (wide_spec,) * 5
    return pl.pallas_call(
        _inproj_kernel,
        grid=(n // tm,),
        in_specs=[
            pl.BlockSpec((tm, d_model), row),
            pl.BlockSpec((1, d_model), const),
            pl.BlockSpec(w.shape, const),
            pl.BlockSpec((tm, LANES), pos),
            pl.BlockSpec((tm, LANES), pos),
            pl.BlockSpec((1, SMALL_W), const),
            pl.BlockSpec((1, SMALL_W), const),
            pl.BlockSpec(lbl.shape, const),
        ],
        out_specs=out_specs,
        out_shape=out_shape,
        compiler_params=pltpu.CompilerParams(
            dimension_semantics=("parallel",), vmem_limit_bytes=VMEM_LIMIT),
        name="inproj",
    )(x2, g, w, cos_t, sin_t, lng, lnb, lbl)


def _meta_proj_kernel(x_ref, g_ref, w_ref, cos_ref, sin_ref, lbl_ref,
                      ka_ref, va_ref, hlf_ref, hk_ref, hv_ref):
    xb = _normed_rows(x_ref, g_ref)
    rope = _rope_fn(cos_ref[...], sin_ref[...], xb.shape[0])

    def proj(g):
        return jnp.dot(xb, w_ref[:, g * 512:(g + 1) * 512], preferred_element_type=f32)

    ka_ref[...] = rope(proj(1)).astype(bf16)
    va_ref[...] = proj(2).astype(bf16)
    hlf_ref[...], hk_ref[...] = _hgrn_gates(lbl_ref, proj(6))
    hv_ref[...] = proj(7)


def _meta_proj(meta, g, w, cos_t, sin_t, lbl):
    n, d_model = meta.shape
    full = lambda a: pl.BlockSpec(a.shape, lambda i: (0,) * a.ndim)
    wide = pl.BlockSpec((n, 512), lambda i: (0, 0))
    return pl.pallas_call(
        _meta_proj_kernel,
        grid=(1,),
        in_specs=[full(meta), full(g), full(w), full(cos_t), full(sin_t), full(lbl)],
        out_specs=(wide,) * 5,
        out_shape=(jax.ShapeDtypeStruct((n, 512), bf16),) * 2
        + (jax.ShapeDtypeStruct((n, 512), f32),) * 3,
        compiler_params=pltpu.CompilerParams(vmem_limit_bytes=VMEM_LIMIT),
        name="meta_proj",
    )(meta, g, w, cos_t, sin_t, lbl)


def _unkey(k):
    return lax.bitcast_convert_type(jnp.where(k < 0, INT_MIN - k, k), f32)


def _attn_kernel(qat_ref, qit_ref, wt_ref, ki_ref, ka_ref, vt_ref, km_ref, vm_ref, o_ref,
                 sc_ref, sb_ref, s_ref, acc_ref, m_ref, thr_ref, *, t, topk):
    i = pl.program_id(1)
    nkb = i + 1
    col_bits = max(1, (sc_ref.shape[0] * t - 1).bit_length())
    key_l = lax.broadcasted_iota(i32, (t, t), 0)
    qry_l = lax.broadcasted_iota(i32, (t, t), 1)

    def score_tile(kb, causal):
        kik = ki_ref[pl.ds(pl.multiple_of(kb * t, t), t), :]
        acc = jnp.zeros((t, t), f32)
        for h in range(IDX_HEADS):
            s = jnp.dot(kik, qit_ref[h * IDX_DIM:(h + 1) * IDX_DIM, :], preferred_element_type=f32)
            acc = acc + jnp.maximum(s, 0.0) * wt_ref[h:h + 1, :]
        if causal:
            acc = jnp.where(key_l <= qry_l, acc, -jnp.inf)
        sc_ref[kb] = acc
        sb_ref[kb] = acc.astype(coarse)

    def score_body(kb, carry):
        score_tile(kb, False)
        return carry

    lax.fori_loop(0, i, score_body, 0)
    score_tile(i, True)

    sub_l = lax.broadcasted_iota(i32, (SUBLANES, t), 0)

    def count(pred):
        def body(kb, accs):
            accs = list(accs)
            for r in range(t // SUBLANES):
                v = sc_ref[kb, r * SUBLANES:(r + 1) * SUBLANES, :]
                accs[r % 4] = accs[r % 4] + jnp.where(pred(v, kb * t + r * SUBLANES + sub_l), 1, 0)
            return tuple(accs)
        z = jnp.zeros((SUBLANES, t), i32)
        a = lax.fori_loop(0, nkb, body, (z, z, z, z))
        return jnp.sum((a[0] + a[1]) + (a[2] + a[3]), axis=0, keepdims=True)

    def count_ge(cand):
        cb = jnp.broadcast_to(cand, (SUBLANES, t))
        return count(lambda v, idx: v >= cb)

    def count_ge_coarse(cand):
        cb = jnp.broadcast_to(cand, (PACK16, t)).astype(coarse)
        one, zero = jnp.int16(1), jnp.int16(0)

        def body(kb, accs):
            accs = list(accs)
            for r in range(t // PACK16):
                v = sb_ref[kb, r * PACK16:(r + 1) * PACK16, :]
                accs[r % 4] = accs[r % 4] + jnp.where(v >= cb, one, zero)
            return tuple(accs)
        z = jnp.zeros((PACK16, t), i16)
        a = lax.fori_loop(0, nkb, body, (z, z, z, z))
        return jnp.sum(((a[0] + a[1]) + (a[2] + a[3])).astype(i32), axis=0, keepdims=True)

    def bisect(count_fn, to_float, carry):
        lo, hi, c_lo, c_hi = carry
        mid = lo + lax.shift_right_logical(hi - lo, 1)
        c = count_fn(to_float(mid))
        up = (c >= topk) & (hi != lo + 1)
        dn = (c < topk) & (hi != lo + 1)
        return (jnp.where(up, mid, lo), jnp.where(dn, mid, hi),
                jnp.where(up, c, c_lo), jnp.where(dn, c, c_hi))

    row = lambda val: jnp.full((1, t), val, i32)
    top_lo, top_hi = KEY_NEG_INF >> 16, KEY_POS_INF >> 16
    top, _, _, _ = lax.fori_loop(
        0, (top_hi - top_lo - 1).bit_length(),
        lambda j, c: bisect(count_ge_coarse, lambda m: _unkey(m * (1 << 16)), c),
        (row(top_lo), row(top_hi), row(0) + nkb * t, row(0)))

    lo0 = jnp.maximum(top - 1, top_lo) * (1 << 16)
    hi0 = jnp.minimum(top + 1, top_hi) * (1 << 16)
    c_hi0 = count_ge(_unkey(hi0))

    def cond(carry):
        it, lo, hi, c_lo, c_hi = carry
        open_ = jnp.where((c_lo != topk) & (hi != lo + 1), 1.0, 0.0)
        return (it < MAX_BISECT) & (jnp.max(open_) > 0.0)

    def step(carry):
        it, state = carry[0], carry[1:]
        for _ in range(BISECT_PER_CHECK):
            state = bisect(count_ge, _unkey, state)
        return (it + BISECT_PER_CHECK,) + state

    _, lo, hi, c_lo, c_hi = lax.while_loop(
        cond, step, (jnp.int32(0), lo0, hi0, row(INT_MAX), c_hi0))
    thr = _unkey(lo)
    thr_ref[...] = jnp.broadcast_to(jnp.maximum(thr, FLT_LOWEST), (SUBLANES, t))

    @pl.when(jnp.max(c_lo) > topk)
    def _():
        tb = jnp.broadcast_to(thr, (SUBLANES, t))
        need = topk - c_hi

        def idx_body(j, jcol):
            cand = jcol | (1 << (col_bits - 1 - j))
            cb = jnp.broadcast_to(cand, (SUBLANES, t))
            c = count(lambda v, idx: (v == tb) & (idx < cb))
            return jnp.where(c < need, cand, jcol)

        jcol = lax.fori_loop(0, col_bits, idx_body, jnp.zeros((1, t), i32))
        jcol = jnp.where(c_lo > topk, jcol, INT_MAX)

        def drop_body(kb, carry):
            v = sc_ref[kb]
            sc_ref[kb] = jnp.where((v == thr) & (kb * t + key_l > jcol), -jnp.inf, v)
            return carry

        lax.fori_loop(0, nkb, drop_body, 0)


    row_h = lax.broadcasted_iota(i32, (2 * HEAD_DIM, t), 0)
    first = row_h < HEAD_DIM

    def q_head(h):
        qp = qat_ref[(h // 2) * LANES:(h // 2 + 1) * LANES, :]
        return jnp.where(first if h % 2 == 0 else ~first, qp, jnp.zeros_like(qp))

    for h in range(ATTN_HEADS):
        cols = slice((h // 2) * LANES, (h // 2 + 1) * LANES)
        s = jnp.dot(km_ref[:, cols], q_head(h), preferred_element_type=f32)
        m0 = jnp.max(s, axis=0, keepdims=True)
        p = jnp.exp2(s - m0).astype(bf16)
        pv = lax.dot_general(vm_ref[:, cols], p, (((0,), (0,)), ((), ())), preferred_element_type=f32)
        pv = pv[(h % 2) * HEAD_DIM:(h % 2 + 1) * HEAD_DIM]
        l0 = jnp.sum(p.astype(f32), axis=0, keepdims=True)
        acc_ref[h] = jnp.concatenate([pv, jnp.broadcast_to(l0, (ONES_ROWS, t))], axis=0)
        m_ref[h] = jnp.broadcast_to(m0, (SUBLANES, t))

    thr_b = thr_ref[0:1, :]

    def attn_body(kb, carry):
        r0 = pl.multiple_of(kb * t, t)
        bias = jnp.where(sc_ref[kb] >= thr_b, 0.0, -jnp.inf)
        alphas, m_news = [], []
        for h in range(ATTN_HEADS):
            cols = slice((h // 2) * LANES, (h // 2 + 1) * LANES)
            s = jnp.dot(ka_ref[pl.ds(r0, t), cols], q_head(h), preferred_element_type=f32) + bias
            s_ref[h] = s
            m_old = m_ref[h][0:1, :]
            m_new = jnp.maximum(m_old, jnp.max(s, axis=0, keepdims=True))
            m_ref[h] = jnp.broadcast_to(m_new, (SUBLANES, t))
            alphas.append(jnp.exp2(m_old - m_new))
            m_news.append(m_new)
        for h in range(ATTN_HEADS):
            p = jnp.exp2(s_ref[h] - m_news[h]).astype(bf16)
            pv = jnp.dot(vt_ref[kb, h * VT_ROWS:(h + 1) * VT_ROWS, :], p, preferred_element_type=f32)
            acc_ref[h] = alphas[h] * acc_ref[h] + pv
        return carry

    lax.fori_loop(0, nkb, attn_body, 0)

    outs = []
    for h in range(ATTN_HEADS):
        a = acc_ref[h]
        outs.append(a[:HEAD_DIM] / a[HEAD_DIM:HEAD_DIM + 1])
    o_ref[...] = jnp.concatenate(outs, axis=0).T.astype(bf16)


def _sparse_attention(qat, qit, wt, ki, ka, vt, km, vm, batch, seq):
    t = TILE
    n = ka.shape[0]
    nq = seq // t
    topk = min(TOPK_MAX, seq // 4)
    qcol = lambda b, i: (0, b * nq + i)
    brow = lambda b, i: (b, 0)
    const = lambda b, i: (0, 0)
    once = pl.Buffered(1)
    return pl.pallas_call(
        functools.partial(_attn_kernel, t=t, topk=topk),
        grid=(batch, nq),
        in_specs=[
            pl.BlockSpec((ATTN_WIDTH, t), qcol),
            pl.BlockSpec((ATTN_WIDTH, t), qcol),
            pl.BlockSpec((IDX_HEADS, t), qcol),
            pl.BlockSpec((seq, IDX_DIM), brow, pipeline_mode=once),
            pl.BlockSpec((seq, ATTN_WIDTH), brow, pipeline_mode=once),
            pl.BlockSpec((nq, ATTN_HEADS * VT_ROWS, t), lambda b, i: (b, 0, 0), pipeline_mode=once),
            pl.BlockSpec((N_META, ATTN_WIDTH), const),
            pl.BlockSpec((N_META, ATTN_WIDTH), const),
        ],
        out_specs=pl.BlockSpec((t, ATTN_WIDTH), lambda b, i: (b * nq + i, 0)),
        out_shape=jax.ShapeDtypeStruct((n, ATTN_WIDTH), bf16),
        scratch_shapes=[
            pltpu.VMEM((nq, t, t), f32),
            pltpu.VMEM((nq, t, t), coarse),
            pltpu.VMEM((ATTN_HEADS, t, t), f32),
            pltpu.VMEM((ATTN_HEADS, VT_ROWS, t), f32),
            pltpu.VMEM((ATTN_HEADS, SUBLANES, t), f32),
            pltpu.VMEM((SUBLANES, t), f32),
        ],
        compiler_params=pltpu.CompilerParams(
            dimension_semantics=("parallel", "arbitrary"), vmem_limit_bytes=VMEM_LIMIT),
        name="sparse_attn",
    )(qat, qit, wt, ki, ka, vt, km, vm)


def _hgrn_chunk(q, lf, k, v, st, tri, band_masks, blk_lower):
    b = jnp.dot(tri, lf, preferred_element_type=f32, precision=lax.Precision.HIGHEST) * LOG2E
    bl = b[CHUNK - 1:CHUNK, :]
    q_in = (q * jnp.exp2(b)).astype(bf16)
    o = lax.dot_general(q_in, st.astype(bf16), (((1,), (1,)), ((), ())), preferred_element_type=f32)

    blocks = [jnp.zeros((SUB, CHUNK), f32)]
    for si in range(1, CHUNK // SUB):
        bref = b[si * SUB - 1:si * SUB, :]
        rows = slice(si * SUB, (si + 1) * SUB)
        q_s = (q[rows] * jnp.exp2(b[rows] - bref)).astype(bf16)
        k_s = (k * jnp.exp2(jnp.minimum(bref - b, 0.0))).astype(bf16)
        blocks.append(lax.dot_general(q_s, k_s, (((1,), (1,)), ((), ())), preferred_element_type=f32))
    a = jnp.where(blk_lower, jnp.concatenate(blocks, axis=0), 0.0)

    for dlt in range(SUB):
        if dlt == 0:
            w = q * k
        else:
            decay = jnp.exp2(jnp.minimum(b - pltpu.roll(b, dlt, 0), 0.0))
            w = q * pltpu.roll(k, dlt, 0) * decay
        a = jnp.where(band_masks[dlt], jnp.sum(w, axis=1, keepdims=True), a)

    o = o + jnp.dot(a.astype(bf16), v.astype(bf16), preferred_element_type=f32)
    k_end = (k * jnp.exp2(bl - b)).astype(bf16)
    upd = lax.dot_general(v.astype(bf16), k_end, (((0,), (0,)), ((), ())), preferred_element_type=f32)
    return o, st * jnp.exp2(bl) + upd


def _hgrn_kernel(q_ref, lf_ref, k_ref, v_ref, lfm_ref, km_ref, vm_ref, o_ref, st_ref, *, n_chunks):
    c = pl.program_id(1)
    r_i = lax.broadcasted_iota(i32, (CHUNK, CHUNK), 0)
    c_i = lax.broadcasted_iota(i32, (CHUNK, CHUNK), 1)
    tri = jnp.where(c_i <= r_i, 1.0, 0.0).astype(f32)
    sub_shift = SUB.bit_length() - 1
    blk_lower = (c_i >> sub_shift) < (r_i >> sub_shift)
    band_masks = [((r_i & (SUB - 1)) >= d) & (c_i == r_i - d) for d in range(SUB)]

    @pl.when(c == 0)
    def _():
        pad = jnp.zeros((CHUNK - N_META, HGRN_DIM), f32)
        for h in range(HGRN_HEADS):
            cols = slice(h * HGRN_DIM, (h + 1) * HGRN_DIM)
            lf = jnp.concatenate([pad, lfm_ref[:, cols]], axis=0)
            k = jnp.concatenate([pad, km_ref[:, cols]], axis=0)
            v = jnp.concatenate([pad, vm_ref[:, cols]], axis=0)
            _, st = _hgrn_chunk(jnp.zeros((CHUNK, HGRN_DIM), f32), lf, k, v,
                                jnp.zeros((HGRN_DIM, HGRN_DIM), f32), tri, band_masks, blk_lower)
            st_ref[h] = st

    def chunk_body(ci, carry):
        rows = pl.ds(pl.multiple_of(ci * CHUNK, CHUNK), CHUNK)
        for h in range(HGRN_HEADS):
            cols = slice(h * HGRN_DIM, (h + 1) * HGRN_DIM)
            o, st = _hgrn_chunk(q_ref[rows, cols], lf_ref[rows, cols], k_ref[rows, cols],
                                v_ref[rows, cols], st_ref[h], tri, band_masks, blk_lower)
            o_ref[rows, cols] = o.astype(bf16)
            st_ref[h] = st
        return carry

    lax.fori_loop(0, n_chunks, chunk_body, 0, unroll=4)


def _hgrn(hq, hlf, hk, hv, lfm, km, vm, batch, seq, tc):
    n = hq.shape[0]
    nc = seq // tc
    row = lambda b, c: (b * nc + c, 0)
    const = lambda b, c: (0, 0)
    spec = pl.BlockSpec((tc, HGRN_WIDTH), row)
    mspec = pl.BlockSpec((N_META, HGRN_WIDTH), const)
    return pl.pallas_call(
        functools.partial(_hgrn_kernel, n_chunks=tc // CHUNK),
        grid=(batch, nc),
        in_specs=[spec, spec, spec, spec, mspec, mspec, mspec],
        out_specs=spec,
        out_shape=jax.ShapeDtypeStruct((n, HGRN_WIDTH), bf16),
        scratch_shapes=[pltpu.VMEM((HGRN_HEADS, HGRN_DIM, HGRN_DIM), f32)],
        compiler_params=pltpu.CompilerParams(
            dimension_semantics=("parallel", "arbitrary"), vmem_limit_bytes=VMEM_LIMIT),
        name="hgrn2",
    )(hq, hlf, hk, hv, lfm, km, vm)


def _outproj_kernel(oa_ref, ga_ref, oh_ref, hg_ref, x_ref, w_ref, na_ref, nh_ref, nf_ref, y_ref):
    oa = oa_ref[...].astype(f32)
    a = (oa * lax.rsqrt(jnp.mean(oa * oa, axis=-1, keepdims=True) + EPS) * na_ref[...]
         * ga_ref[...].astype(f32))
    parts = []
    for h in range(HGRN_HEADS):
        cols = slice(h * HGRN_DIM, (h + 1) * HGRN_DIM)
        oh = oh_ref[:, cols].astype(f32)
        nrm = oh * lax.rsqrt(jnp.mean(oh * oh, axis=-1, keepdims=True) + EPS) * nh_ref[...]
        parts.append(nrm * hg_ref[:, cols].astype(f32))
    hh = jnp.concatenate(parts, axis=-1)
    mix = (jnp.dot(a.astype(bf16), w_ref[:ATTN_WIDTH, :], preferred_element_type=f32)
           + jnp.dot(hh.astype(bf16), w_ref[ATTN_WIDTH:, :], preferred_element_type=f32))
    hres = x_ref[...] + mix
    y_ref[...] = hres * lax.rsqrt(jnp.mean(hres * hres, axis=-1, keepdims=True) + EPS) * nf_ref[...]


def _outproj(oa, ga, oh, hg, x2, w, na, nh, nf, tm):
    n, d_model = x2.shape
    row = lambda i: (i, 0)
    const = lambda i: (0, 0)
    half = pl.BlockSpec((tm, 512), row)
    return pl.pallas_call(
        _outproj_kernel,
        grid=(n // tm,),
        in_specs=[half, half, half, half,
                  pl.BlockSpec((tm, d_model), row),
                  pl.BlockSpec(w.shape, const),
                  pl.BlockSpec((1, ATTN_WIDTH), const),
                  pl.BlockSpec((1, HGRN_DIM), const),
                  pl.BlockSpec((1, d_model), const)],
        out_specs=pl.BlockSpec((tm, d_model), row),
        out_shape=jax.ShapeDtypeStruct((n, d_model), f32),
        compiler_params=pltpu.CompilerParams(
            dimension_semantics=("parallel",), vmem_limit_bytes=VMEM_LIMIT),
        name="outproj",
    )(oa, ga, oh, hg, x2, w, na, nh, nf)


def _rope_tables(n_pos):
    inv = ROPE_THETA ** (-jnp.arange(ROPE_HALF, dtype=f32) * 2.0 / ROPE_DIMS)
    ang = jnp.arange(n_pos, dtype=f32)[:, None] * inv[None, :]
    cos, sin = jnp.cos(ang), jnp.sin(ang)
    rest = HEAD_DIM - ROPE_DIMS
    cos_h = jnp.concatenate([cos, cos, jnp.ones((n_pos, rest), f32)], axis=-1)
    sin_h = jnp.concatenate([-sin, sin, jnp.zeros((n_pos, rest), f32)], axis=-1)
    return jnp.tile(cos_h, (1, LANES // HEAD_DIM)), jnp.tile(sin_h, (1, LANES // HEAD_DIM))


def kernel(x, meta_tokens, mix_norm, w_in, idx_k_norm_g, idx_k_norm_b, attn_out_norm,
           hgrn_lb_logits, hgrn_out_norm, w_out, final_norm):
    batch, seq, d_model = x.shape
    assert seq % TILE == 0
    x2 = x.reshape(batch * seq, d_model)

    wl = w_in[0]
    a_end = 4 * ATTN_WIDTH
    iq_end = a_end + IDX_HEADS * IDX_DIM
    sm_end = iq_end + IDX_DIM + IDX_HEADS
    w_small = jnp.pad(wl[:, iq_end:sm_end], ((0, 0), (0, SMALL_W - (sm_end - iq_end))))
    w = jnp.concatenate([wl[:, :iq_end], wl[:, sm_end:], w_small], axis=-1).astype(bf16)

    cos_t, sin_t = _rope_tables(N_META + seq)
    g = mix_norm[0][None, :]
    lng = jnp.pad(idx_k_norm_g[0], (0, SMALL_W - IDX_DIM))[None, :]
    lnb = jnp.pad(idx_k_norm_b[0], (0, SMALL_W - IDX_DIM))[None, :]

    qat, ka, vt, ga, qit, ki, wt, hq, hlf, hk, hv, hg = _inproj(
        x2, g, w, cos_t[N_META:], sin_t[N_META:], lng, lnb, hgrn_lb_logits, seq // TILE)
    ka_m, va_m, hlf_m, hk_m, hv_m = _meta_proj(
        meta_tokens, g, w, cos_t[:N_META], sin_t[:N_META], hgrn_lb_logits)

    oa = _sparse_attention(qat, qit, wt, ki, ka, vt, ka_m, va_m, batch, seq)
    oh = _hgrn(hq, hlf, hk, hv, hlf_m, hk_m, hv_m, batch, seq, tc=512)
    y = _outproj(oa, ga, oh, hg, x2, w_out[0].astype(bf16), attn_out_norm[0][None, :],
                 hgrn_out_norm[0][None, :], final_norm[None, :], tm=256)
    return y.reshape(batch, seq, d_model)
```

```python
import functools
import math

import jax
import jax.numpy as jnp
from jax import lax
from jax.experimental import pallas as pl
from jax.experimental.pallas import tpu as pltpu

N_META = 16
ATTN_HEADS = 8
HEAD_DIM = 64
ATTN_WIDTH = ATTN_HEADS * HEAD_DIM
IDX_HEADS = 8
IDX_DIM = 64
IDX_SCALE = (IDX_HEADS ** -0.5) * (IDX_DIM ** -0.5)
ATTN_SCALE = HEAD_DIM ** -0.5
LOG2E = math.log2(math.e)
TOPK_MAX = 256
HGRN_HEADS = 4
HGRN_DIM = 128
HGRN_WIDTH = HGRN_HEADS * HGRN_DIM
CHUNK = 64
SUB = 16
ROPE_THETA = 500000.0
ROPE_DIMS = HEAD_DIM // 4
ROPE_HALF = ROPE_DIMS // 2
EPS = 1e-6

LANES = 128
SUBLANES = 8
VMEM_LIMIT = 56 * 1024 * 1024

TILE = 256
N_WIDE = 9
SMALL_W = 128
ONES_ROWS = 16
VT_ROWS = HEAD_DIM + ONES_ROWS
INT_MAX = 2 ** 31 - 1
INT_MIN = -(2 ** 31)
KEY_POS_INF = 0x7F800000
KEY_NEG_INF = -KEY_POS_INF
PACK16 = 16
FLT_LOWEST = -3.4028234663852886e38
MAX_BISECT = 20
BISECT_PER_CHECK = 4

f32 = jnp.float32
bf16 = jnp.bfloat16
i32 = jnp.int32
i16 = jnp.int16
coarse = jnp.bfloat16


def _sigmoid(v):
    return 1.0 / (1.0 + jnp.exp(-v))


def _normed_rows(x_ref, g_ref):
    x = x_ref[...]
    ms = jnp.mean(x * x, axis=-1, keepdims=True)
    return ((x * lax.rsqrt(ms + EPS)) * g_ref[...]).astype(bf16)


def _rope_fn(cos1, sin1, tm):
    cos = jnp.concatenate([cos1] * 4, axis=-1)
    sin = jnp.concatenate([sin1] * 4, axis=-1)
    lane = lax.broadcasted_iota(i32, (tm, ATTN_WIDTH), 1)
    low = (lane & (HEAD_DIM - 1)) < ROPE_HALF

    def rope(v):
        partner = jnp.where(low, pltpu.roll(v, ATTN_WIDTH - ROPE_HALF, 1),
                            pltpu.roll(v, ROPE_HALF, 1))
        return v * cos + partner * sin
    return rope


def _hgrn_gates(lbl_ref, hf):
    lg = lbl_ref[...]
    e = jnp.exp(lg - jnp.max(lg, axis=0, keepdims=True))
    lb = e[0:1, :] / jnp.sum(e, axis=0, keepdims=True)
    f = lb + (1.0 - lb) * _sigmoid(hf)
    return jnp.log(f), 1.0 - f


def _inproj_kernel(x_ref, g_ref, w_ref, cos_ref, sin_ref, lng_ref, lnb_ref, lbl_ref,
                   qat_ref, ka_ref, vt_ref, ga_ref, qit_ref, ki_ref, wt_ref,
                   hq_ref, hlf_ref, hk_ref, hv_ref, hg_ref):
    xb = _normed_rows(x_ref, g_ref)
    tm = xb.shape[0]
    cos1 = cos_ref[...]
    sin1 = sin_ref[...]
    rope = _rope_fn(cos1, sin1, tm)

    def proj(g):
        return jnp.dot(xb, w_ref[:, g * 512:(g + 1) * 512], preferred_element_type=f32)

    qat_ref[...] = (rope(proj(0)) * (ATTN_SCALE * LOG2E)).T.astype(bf16)
    ka_ref[...] = rope(proj(1)).astype(bf16)
    vt = proj(2).T
    ones = jnp.ones((ONES_ROWS, tm), f32)
    parts = []
    for h in range(ATTN_HEADS):
        parts += [vt[h * HEAD_DIM:(h + 1) * HEAD_DIM], ones]
    vt_ref[0] = jnp.concatenate(parts, axis=0).astype(bf16)
    ag = proj(3)
    ga_ref[...] = (ag * _sigmoid(ag)).astype(bf16)
    qit_ref[...] = rope(proj(4)).T.astype(bf16)

    hq = proj(5)
    hq_ref[...] = hq * _sigmoid(hq)
    hlf_ref[...], hk_ref[...] = _hgrn_gates(lbl_ref, proj(6))
    hv_ref[...] = proj(7)
    hg = proj(8)
    hg_ref[...] = (hg * _sigmoid(hg)).astype(bf16)

    sm = jnp.dot(xb, w_ref[:, N_WIDE * 512:N_WIDE * 512 + SMALL_W], preferred_element_type=f32)
    lane1 = lax.broadcasted_iota(i32, (tm, SMALL_W), 1)
    is_key = lane1 < IDX_DIM
    mu = jnp.sum(jnp.where(is_key, sm, 0.0), axis=-1, keepdims=True) / IDX_DIM
    d = sm - mu
    var = jnp.sum(jnp.where(is_key, d * d, 0.0), axis=-1, keepdims=True) / IDX_DIM
    y = d * lax.rsqrt(var + EPS) * lng_ref[...] + lnb_ref[...]
    low1 = (lane1 & (IDX_DIM - 1)) < ROPE_HALF
    partner = jnp.where(low1, pltpu.roll(y, SMALL_W - ROPE_HALF, 1), pltpu.roll(y, ROPE_HALF, 1))
    y = y * cos1 + partner * sin1
    ki_ref[...] = y[:, :IDX_DIM].astype(bf16)
    wt_ref[...] = sm.T[IDX_DIM:IDX_DIM + IDX_HEADS] * IDX_SCALE


def _inproj(x2, g, w, cos_t, sin_t, lng, lnb, lbl, n_pos_blocks):
    n, d_model = x2.shape
    tm = TILE
    row = lambda i: (i, 0)
    col = lambda i: (0, i)
    const = lambda i: (0, 0)
    pos = lambda i: (i % n_pos_blocks, 0)
    wide_bf = jax.ShapeDtypeStruct((n, 512), bf16)
    wide_f = jax.ShapeDtypeStruct((n, 512), f32)
    wide_t = jax.ShapeDtypeStruct((512, n), bf16)
    out_shape = (wide_t, wide_bf,
                 jax.ShapeDtypeStruct((n // tm, ATTN_HEADS * VT_ROWS, tm), bf16),
                 wide_bf, wide_t,
                 jax.ShapeDtypeStruct((n, IDX_DIM), bf16),
                 jax.ShapeDtypeStruct((IDX_HEADS, n), f32),
                 wide_f, wide_f, wide_f, wide_f, wide_bf)
    wide_spec = pl.BlockSpec((tm, 512), row)
    t_spec = pl.BlockSpec((512, tm), col)
    out_specs = (t_spec, wide_spec,
                 pl.BlockSpec((1, ATTN_HEADS * VT_ROWS, tm), lambda i: (i, 0, 0)),
                 wide_spec, t_spec,
                 pl.BlockSpec((tm, IDX_DIM), row),
                 pl.BlockSpec((IDX_HEADS, tm), col)) + (wide_spec,) * 5
    return pl.pallas_call(
        _inproj_kernel,
        grid=(n // tm,),
        in_specs=[
            pl.BlockSpec((tm, d_model), row),
            pl.BlockSpec((1, d_model), const),
            pl.BlockSpec(w.shape, const),
            pl.BlockSpec((tm, LANES), pos),
            pl.BlockSpec((tm, LANES), pos),
            pl.BlockSpec((1, SMALL_W), const),
            pl.BlockSpec((1, SMALL_W), const),
            pl.BlockSpec(lbl.shape, const),
        ],
        out_specs=out_specs,
        out_shape=out_shape,
        compiler_params=pltpu.CompilerParams(
            dimension_semantics=("parallel",), vmem_limit_bytes=VMEM_LIMIT),
        name="inproj",
    )(x2, g, w, cos_t, sin_t, lng, lnb, lbl)


def _meta_proj_kernel(x_ref, g_ref, w_ref, cos_ref, sin_ref, lbl_ref,
                      ka_ref, va_ref, hlf_ref, hk_ref, hv_ref):
    xb = _normed_rows(x_ref, g_ref)
    rope = _rope_fn(cos_ref[...], sin_ref[...], xb.shape[0])

    def proj(g):
        return jnp.dot(xb, w_ref[:, g * 512:(g + 1) * 512], preferred_element_type=f32)

    ka_ref[...] = rope(proj(1)).astype(bf16)
    va_ref[...] = proj(2).astype(bf16)
    hlf_ref[...], hk_ref[...] = _hgrn_gates(lbl_ref, proj(6))
    hv_ref[...] = proj(7)


def _meta_proj(meta, g, w, cos_t, sin_t, lbl):
    n, d_model = meta.shape
    full = lambda a: pl.BlockSpec(a.shape, lambda i: (0,) * a.ndim)
    wide = pl.BlockSpec((n, 512), lambda i: (0, 0))
    return pl.pallas_call(
        _meta_proj_kernel,
        grid=(1,),
        in_specs=[full(meta), full(g), full(w), full(cos_t), full(sin_t), full(lbl)],
        out_specs=(wide,) * 5,
        out_shape=(jax.ShapeDtypeStruct((n, 512), bf16),) * 2
        + (jax.ShapeDtypeStruct((n, 512), f32),) * 3,
        compiler_params=pltpu.CompilerParams(vmem_limit_bytes=VMEM_LIMIT),
        name="meta_proj",
    )(meta, g, w, cos_t, sin_t, lbl)


def _unkey(k):
    return lax.bitcast_convert_type(jnp.where(k < 0, INT_MIN - k, k), f32)


def _attn_kernel(qat_ref, qit_ref, wt_ref, ki_ref, ka_ref, vt_ref, km_ref, vm_ref, o_ref,
                 sc_ref, sb_ref, s_ref, acc_ref, m_ref, thr_ref, *, t, topk):
    i = pl.program_id(1)
    nkb = i + 1
    col_bits = max(1, (sc_ref.shape[0] * t - 1).bit_length())
    key_l = lax.broadcasted_iota(i32, (t, t), 0)
    qry_l = lax.broadcasted_iota(i32, (t, t), 1)

    def score_tile(kb, causal):
        kik = ki_ref[pl.ds(pl.multiple_of(kb * t, t), t), :]
        acc = jnp.zeros((t, t), f32)
        for h in range(IDX_HEADS):
            s = jnp.dot(kik, qit_ref[h * IDX_DIM:(h + 1) * IDX_DIM, :], preferred_element_type=f32)
            acc = acc + jnp.maximum(s, 0.0) * wt_ref[h:h + 1, :]
        if causal:
            acc = jnp.where(key_l <= qry_l, acc, -jnp.inf)
        sc_ref[kb] = acc
        sb_ref[kb] = acc.astype(coarse)

    def score_body(j, carry):
        score_tile(2 * j, False)
        score_tile(2 * j + 1, False)
        return carry

    lax.fori_loop(0, i // 2, score_body, 0)

    @pl.when(i % 2 == 1)
    def _():
        score_tile(i - 1, False)

    score_tile(i, True)

    sub_l = lax.broadcasted_iota(i32, (SUBLANES, t), 0)

    def count(pred):
        def body(kb, accs):
            accs = list(accs)
            for r in range(t // SUBLANES):
                v = sc_ref[kb, r * SUBLANES:(r + 1) * SUBLANES, :]
                accs[r % 4] = accs[r % 4] + jnp.where(pred(v, kb * t + r * SUBLANES + sub_l), 1, 0)
            return tuple(accs)
        z = jnp.zeros((SUBLANES, t), i32)
        a = lax.fori_loop(0, nkb, body, (z, z, z, z))
        return jnp.sum((a[0] + a[1]) + (a[2] + a[3]), axis=0, keepdims=True)

    def count_ge(cand):
        cb = jnp.broadcast_to(cand, (SUBLANES, t))
        return count(lambda v, idx: v >= cb)

    def count_ge_coarse(cand):
        cb = jnp.broadcast_to(cand, (PACK16, t)).astype(coarse)
        one, zero = jnp.int16(1), jnp.int16(0)

        def body(kb, accs):
            accs = list(accs)
            for r in range(t // PACK16):
                v = sb_ref[kb, r * PACK16:(r + 1) * PACK16, :]
                accs[r % 4] = accs[r % 4] + jnp.where(v >= cb, one, zero)
            return tuple(accs)
        z = jnp.zeros((PACK16, t), i16)
        a = lax.fori_loop(0, nkb, body, (z, z, z, z))
        return jnp.sum(((a[0] + a[1]) + (a[2] + a[3])).astype(i32), axis=0, keepdims=True)

    def bisect(count_fn, to_float, carry):
        lo, hi, c_lo, c_hi = carry
        mid = lo + lax.shift_right_logical(hi - lo, 1)
        c = count_fn(to_float(mid))
        up = (c >= topk) & (hi != lo + 1)
        dn = (c < topk) & (hi != lo + 1)
        return (jnp.where(up, mid, lo), jnp.where(dn, mid, hi),
                jnp.where(up, c, c_lo), jnp.where(dn, c, c_hi))

    row = lambda val: jnp.full((1, t), val, i32)
    top_lo, top_hi = KEY_NEG_INF >> 16, KEY_POS_INF >> 16
    top, _, _, _ = lax.fori_loop(
        0, (top_hi - top_lo - 1).bit_length(),
        lambda j, c: bisect(count_ge_coarse, lambda m: _unkey(m * (1 << 16)), c),
        (row(top_lo), row(top_hi), row(0) + nkb * t, row(0)))

    lo0 = jnp.maximum(top - 1, top_lo) * (1 << 16)
    hi0 = jnp.minimum(top + 1, top_hi) * (1 << 16)
    c_hi0 = count_ge(_unkey(hi0))

    def cond(carry):
        it, lo, hi, c_lo, c_hi = carry
        open_ = jnp.where((c_lo != topk) & (hi != lo + 1), 1.0, 0.0)
        return (it < MAX_BISECT) & (jnp.max(open_) > 0.0)

    def step(carry):
        it, state = carry[0], carry[1:]
        for _ in range(BISECT_PER_CHECK):
            state = bisect(count_ge, _unkey, state)
        return (it + BISECT_PER_CHECK,) + state

    _, lo, hi, c_lo, c_hi = lax.while_loop(
        cond, step, (jnp.int32(0), lo0, hi0, row(INT_MAX), c_hi0))
    thr = _unkey(lo)
    thr_ref[...] = jnp.broadcast_to(jnp.maximum(thr, FLT_LOWEST), (SUBLANES, t))

    @pl.when(jnp.max(c_lo) > topk)
    def _():
        tb = jnp.broadcast_to(thr, (SUBLANES, t))
        need = topk - c_hi

        def idx_body(j, jcol):
            cand = jcol | (1 << (col_bits - 1 - j))
            cb = jnp.broadcast_to(cand, (SUBLANES, t))
            c = count(lambda v, idx: (v == tb) & (idx < cb))
            return jnp.where(c < need, cand, jcol)

        jcol = lax.fori_loop(0, col_bits, idx_body, jnp.zeros((1, t), i32))
        jcol = jnp.where(c_lo > topk, jcol, INT_MAX)

        def drop_body(kb, carry):
            v = sc_ref[kb]
            sc_ref[kb] = jnp.where((v == thr) & (kb * t + key_l > jcol), -jnp.inf, v)
            return carry

        lax.fori_loop(0, nkb, drop_body, 0)


    row_h = lax.broadcasted_iota(i32, (2 * HEAD_DIM, t), 0)
    first = row_h < HEAD_DIM

    def q_head(h):
        qp = qat_ref[(h // 2) * LANES:(h // 2 + 1) * LANES, :]
        return jnp.where(first if h % 2 == 0 else ~first, qp, jnp.zeros_like(qp))

    for h in range(ATTN_HEADS):
        cols = slice((h // 2) * LANES, (h // 2 + 1) * LANES)
        s = jnp.dot(km_ref[:, cols], q_head(h), preferred_element_type=f32)
        m0 = jnp.max(s, axis=0, keepdims=True)
        p = jnp.exp2(s - m0).astype(bf16)
        pv = lax.dot_general(vm_ref[:, cols], p, (((0,), (0,)), ((), ())), preferred_element_type=f32)
        pv = pv[(h % 2) * HEAD_DIM:(h % 2 + 1) * HEAD_DIM]
        l0 = jnp.sum(p.astype(f32), axis=0, keepdims=True)
        acc_ref[h] = jnp.concatenate([pv, jnp.broadcast_to(l0, (ONES_ROWS, t))], axis=0)
        m_ref[h] = jnp.broadcast_to(m0, (SUBLANES, t))

    thr_b = thr_ref[0:1, :]

    def attn_body(kb, carry):
        r0 = pl.multiple_of(kb * t, t)
        bias = jnp.where(sc_ref[kb] >= thr_b, 0.0, -jnp.inf)
        alphas, m_news = [], []
        for h in range(ATTN_HEADS):
            cols = slice((h // 2) * LANES, (h // 2 + 1) * LANES)
            s = jnp.dot(ka_ref[pl.ds(r0, t), cols], q_head(h), preferred_element_type=f32) + bias
            s_ref[h] = s
            m_old = m_ref[h][0:1, :]
            m_new = jnp.maximum(m_old, jnp.max(s, axis=0, keepdims=True))
            m_ref[h] = jnp.broadcast_to(m_new, (SUBLANES, t))
            alphas.append(jnp.exp2(m_old - m_new))
            m_news.append(m_new)
        for h in range(ATTN_HEADS):
            p = jnp.exp2(s_ref[h] - m_news[h]).astype(bf16)
            pv = jnp.dot(vt_ref[kb, h * VT_ROWS:(h + 1) * VT_ROWS, :], p, preferred_element_type=f32)
            acc_ref[h] = alphas[h] * acc_ref[h] + pv
        return carry

    def attn_pair(j, carry):
        attn_body(2 * j, carry)
        return attn_body(2 * j + 1, carry)

    lax.fori_loop(0, nkb // 2, attn_pair, 0)

    @pl.when(nkb % 2 == 1)
    def _():
        attn_body(nkb - 1, 0)

    outs = []
    for h in range(ATTN_HEADS):
        a = acc_ref[h]
        outs.append(a[:HEAD_DIM] / a[HEAD_DIM:HEAD_DIM + 1])
    o_ref[...] = jnp.concatenate(outs, axis=0).T.astype(bf16)


def _sparse_attention(qat, qit, wt, ki, ka, vt, km, vm, batch, seq):
    t = TILE
    n = ka.shape[0]
    nq = seq // t
    topk = min(TOPK_MAX, seq // 4)
    qcol = lambda b, i: (0, b * nq + i)
    brow = lambda b, i: (b, 0)
    const = lambda b, i: (0, 0)
    once = pl.Buffered(1)
    return pl.pallas_call(
        functools.partial(_attn_kernel, t=t, topk=topk),
        grid=(batch, nq),
        in_specs=[
            pl.BlockSpec((ATTN_WIDTH, t), qcol),
            pl.BlockSpec((ATTN_WIDTH, t), qcol),
            pl.BlockSpec((IDX_HEADS, t), qcol),
            pl.BlockSpec((seq, IDX_DIM), brow, pipeline_mode=once),
            pl.BlockSpec((seq, ATTN_WIDTH), brow, pipeline_mode=once),
            pl.BlockSpec((nq, ATTN_HEADS * VT_ROWS, t), lambda b, i: (b, 0, 0), pipeline_mode=once),
            pl.BlockSpec((N_META, ATTN_WIDTH), const),
            pl.BlockSpec((N_META, ATTN_WIDTH), const),
        ],
        out_specs=pl.BlockSpec((t, ATTN_WIDTH), lambda b, i: (b * nq + i, 0)),
        out_shape=jax.ShapeDtypeStruct((n, ATTN_WIDTH), bf16),
        scratch_shapes=[
            pltpu.VMEM((nq, t, t), f32),
            pltpu.VMEM((nq, t, t), coarse),
            pltpu.VMEM((ATTN_HEADS, t, t), f32),
            pltpu.VMEM((ATTN_HEADS, VT_ROWS, t), f32),
            pltpu.VMEM((ATTN_HEADS, SUBLANES, t), f32),
            pltpu.VMEM((SUBLANES, t), f32),
        ],
        compiler_params=pltpu.CompilerParams(
            dimension_semantics=("parallel", "arbitrary"), vmem_limit_bytes=VMEM_LIMIT),
        name="sparse_attn",
    )(qat, qit, wt, ki, ka, vt, km, vm)


def _hgrn_chunk(q, lf, k, v, st, tri, band_masks, blk_lower):
    b = jnp.dot(tri, lf, preferred_element_type=f32, precision=lax.Precision.HIGHEST) * LOG2E
    bl = b[CHUNK - 1:CHUNK, :]
    q_in = (q * jnp.exp2(b)).astype(bf16)
    o = lax.dot_general(q_in, st.astype(bf16), (((1,), (1,)), ((), ())), preferred_element_type=f32)

    blocks = [jnp.zeros((SUB, CHUNK), f32)]
    for si in range(1, CHUNK // SUB):
        bref = b[si * SUB - 1:si * SUB, :]
        rows = slice(si * SUB, (si + 1) * SUB)
        q_s = (q[rows] * jnp.exp2(b[rows] - bref)).astype(bf16)
        k_s = (k * jnp.exp2(jnp.minimum(bref - b, 0.0))).astype(bf16)
        blocks.append(lax.dot_general(q_s, k_s, (((1,), (1,)), ((), ())), preferred_element_type=f32))
    a = jnp.where(blk_lower, jnp.concatenate(blocks, axis=0), 0.0)

    for dlt in range(SUB):
        if dlt == 0:
            w = q * k
        else:
            decay = jnp.exp2(jnp.minimum(b - pltpu.roll(b, dlt, 0), 0.0))
            w = q * pltpu.roll(k, dlt, 0) * decay
        a = jnp.where(band_masks[dlt], jnp.sum(w, axis=1, keepdims=True), a)

    o = o + jnp.dot(a.astype(bf16), v.astype(bf16), preferred_element_type=f32)
    k_end = (k * jnp.exp2(bl - b)).astype(bf16)
    upd = lax.dot_general(v.astype(bf16), k_end, (((0,), (0,)), ((), ())), preferred_element_type=f32)
    return o, st * jnp.exp2(bl) + upd


def _hgrn_kernel(q_ref, lf_ref, k_ref, v_ref, lfm_ref, km_ref, vm_ref, o_ref, st_ref, *, n_chunks):
    c = pl.program_id(1)
    r_i = lax.broadcasted_iota(i32, (CHUNK, CHUNK), 0)
    c_i = lax.broadcasted_iota(i32, (CHUNK, CHUNK), 1)
    tri = jnp.where(c_i <= r_i, 1.0, 0.0).astype(f32)
    sub_shift = SUB.bit_length() - 1
    blk_lower = (c_i >> sub_shift) < (r_i >> sub_shift)
    band_masks = [((r_i & (SUB - 1)) >= d) & (c_i == r_i - d) for d in range(SUB)]

    @pl.when(c == 0)
    def _():
        pad = jnp.zeros((CHUNK - N_META, HGRN_DIM), f32)
        for h in range(HGRN_HEADS):
            cols = slice(h * HGRN_DIM, (h + 1) * HGRN_DIM)
            lf = jnp.concatenate([pad, lfm_ref[:, cols]], axis=0)
            k = jnp.concatenate([pad, km_ref[:, cols]], axis=0)
            v = jnp.concatenate([pad, vm_ref[:, cols]], axis=0)
            _, st = _hgrn_chunk(jnp.zeros((CHUNK, HGRN_DIM), f32), lf, k, v,
                                jnp.zeros((HGRN_DIM, HGRN_DIM), f32), tri, band_masks, blk_lower)
            st_ref[h] = st

    def chunk_body(ci, carry):
        rows = pl.ds(pl.multiple_of(ci * CHUNK, CHUNK), CHUNK)
        for h in range(HGRN_HEADS):
            cols = slice(h * HGRN_DIM, (h + 1) * HGRN_DIM)
            o, st = _hgrn_chunk(q_ref[rows, cols], lf_ref[rows, cols], k_ref[rows, cols],
                                v_ref[rows, cols], st_ref[h], tri, band_masks, blk_lower)
            o_ref[rows, cols] = o.astype(bf16)
            st_ref[h] = st
        return carry

    lax.fori_loop(0, n_chunks, chunk_body, 0, unroll=4)


def _hgrn(hq, hlf, hk, hv, lfm, km, vm, batch, seq, tc):
    n = hq.shape[0]
    nc = seq // tc
    row = lambda b, c: (b * nc + c, 0)
    const = lambda b, c: (0, 0)
    spec = pl.BlockSpec((tc, HGRN_WIDTH), row)
    mspec = pl.BlockSpec((N_META, HGRN_WIDTH), const)
    return pl.pallas_call(
        functools.partial(_hgrn_kernel, n_chunks=tc // CHUNK),
        grid=(batch, nc),
        in_specs=[spec, spec, spec, spec, mspec, mspec, mspec],
        out_specs=spec,
        out_shape=jax.ShapeDtypeStruct((n, HGRN_WIDTH), bf16),
        scratch_shapes=[pltpu.VMEM((HGRN_HEADS, HGRN_DIM, HGRN_DIM), f32)],
        compiler_params=pltpu.CompilerParams(
            dimension_semantics=("parallel", "arbitrary"), vmem_limit_bytes=VMEM_LIMIT),
        name="hgrn2",
    )(hq, hlf, hk, hv, lfm, km, vm)


def _outproj_kernel(oa_ref, ga_ref, oh_ref, hg_ref, x_ref, w_ref, na_ref, nh_ref, nf_ref, y_ref):
    oa = oa_ref[...].astype(f32)
    a = (oa * lax.rsqrt(jnp.mean(oa * oa, axis=-1, keepdims=True) + EPS) * na_ref[...]
         * ga_ref[...].astype(f32))
    parts = []
    for h in range(HGRN_HEADS):
        cols = slice(h * HGRN_DIM, (h + 1) * HGRN_DIM)
        oh = oh_ref[:, cols].astype(f32)
        nrm = oh * lax.rsqrt(jnp.mean(oh * oh, axis=-1, keepdims=True) + EPS) * nh_ref[...]
        parts.append(nrm * hg_ref[:, cols].astype(f32))
    hh = jnp.concatenate(parts, axis=-1)
    mix = (jnp.dot(a.astype(bf16), w_ref[:ATTN_WIDTH, :], preferred_element_type=f32)
           + jnp.dot(hh.astype(bf16), w_ref[ATTN_WIDTH:, :], preferred_element_type=f32))
    hres = x_ref[...] + mix
    y_ref[...] = hres * lax.rsqrt(jnp.mean(hres * hres, axis=-1, keepdims=True) + EPS) * nf_ref[...]


def _outproj(oa, ga, oh, hg, x2, w, na, nh, nf, tm):
    n, d_model = x2.shape
    row = lambda i: (i, 0)
    const = lambda i: (0, 0)
    half = pl.BlockSpec((tm, 512), row)
    return pl.pallas_call(
        _outproj_kernel,
        grid=(n // tm,),
        in_specs=[half, half, half, half,
                  pl.BlockSpec((tm, d_model), row),
                  pl.BlockSpec(w.shape, const),
                  pl.BlockSpec((1, ATTN_WIDTH), const),
                  pl.BlockSpec((1, HGRN_DIM), const),
                  pl.BlockSpec((1, d_model), const)],
        out_specs=pl.BlockSpec((tm, d_model), row),
        out_shape=jax.ShapeDtypeStruct((n, d_model), f32),
        compiler_params=pltpu.CompilerParams(
            dimension_semantics=("parallel",), vmem_limit_bytes=VMEM_LIMIT),
        name="outproj",
    )(oa, ga, oh, hg, x2, w, na, nh, nf)


def _rope_tables(n_pos):
    inv = ROPE_THETA ** (-jnp.arange(ROPE_HALF, dtype=f32) * 2.0 / ROPE_DIMS)
    ang = jnp.arange(n_pos, dtype=f32)[:, None] * inv[None, :]
    cos, sin = jnp.cos(ang), jnp.sin(ang)
    rest = HEAD_DIM - ROPE_DIMS
    cos_h = jnp.concatenate([cos, cos, jnp.ones((n_pos, rest), f32)], axis=-1)
    sin_h = jnp.concatenate([-sin, sin, jnp.zeros((n_pos, rest), f32)], axis=-1)
    return jnp.tile(cos_h, (1, LANES // HEAD_DIM)), jnp.tile(sin_h, (1, LANES // HEAD_DIM))


def kernel(x, meta_tokens, mix_norm, w_in, idx_k_norm_g, idx_k_norm_b, attn_out_norm,
           hgrn_lb_logits, hgrn_out_norm, w_out, final_norm):
    batch, seq, d_model = x.shape
    assert seq % TILE == 0
    x2 = x.reshape(batch * seq, d_model)

    wl = w_in[0]
    a_end = 4 * ATTN_WIDTH
    iq_end = a_end + IDX_HEADS * IDX_DIM
    sm_end = iq_end + IDX_DIM + IDX_HEADS
    w_small = jnp.pad(wl[:, iq_end:sm_end], ((0, 0), (0, SMALL_W - (sm_end - iq_end))))
    w = jnp.concatenate([wl[:, :iq_end], wl[:, sm_end:], w_small], axis=-1).astype(bf16)

    cos_t, sin_t = _rope_tables(N_META + seq)
    g = mix_norm[0][None, :]
    lng = jnp.pad(idx_k_norm_g[0], (0, SMALL_W - IDX_DIM))[None, :]
    lnb = jnp.pad(idx_k_norm_b[0], (0, SMALL_W - IDX_DIM))[None, :]

    qat, ka, vt, ga, qit, ki, wt, hq, hlf, hk, hv, hg = _inproj(
        x2, g, w, cos_t[N_META:], sin_t[N_META:], lng, lnb, hgrn_lb_logits, seq // TILE)
    ka_m, va_m, hlf_m, hk_m, hv_m = _meta_proj(
        meta_tokens, g, w, cos_t[:N_META], sin_t[:N_META], hgrn_lb_logits)

    oa = _sparse_attention(qat, qit, wt, ki, ka, vt, ka_m, va_m, batch, seq)
    oh = _hgrn(hq, hlf, hk, hv, hlf_m, hk_m, hv_m, batch, seq, tc=512)
    y = _outproj(oa, ga, oh, hg, x2, w_out[0].astype(bf16), attn_out_norm[0][None, :],
                 hgrn_out_norm[0][None, :], final_norm[None, :], tm=256)
    return y.reshape(batch, seq, d_model)
```

```python
import functools
import math

import jax
import jax.numpy as jnp
from jax import lax
from jax.experimental import pallas as pl
from jax.experimental.pallas import tpu as pltpu

N_META = 16
ATTN_HEADS = 8
HEAD_DIM = 64
ATTN_WIDTH = ATTN_HEADS * HEAD_DIM
IDX_HEADS = 8
IDX_DIM = 64
IDX_SCALE = (IDX_HEADS ** -0.5) * (IDX_DIM ** -0.5)
ATTN_SCALE = HEAD_DIM ** -0.5
LOG2E = math.log2(math.e)
TOPK_MAX = 256
HGRN_HEADS = 4
HGRN_DIM = 128
HGRN_WIDTH = HGRN_HEADS * HGRN_DIM
CHUNK = 64
SUB = 16
ROPE_THETA = 500000.0
ROPE_DIMS = HEAD_DIM // 4
ROPE_HALF = ROPE_DIMS // 2
EPS = 1e-6

LANES = 128
SUBLANES = 8
VMEM_LIMIT = 56 * 1024 * 1024

TILE = 256
N_WIDE = 9
SMALL_W = 128
ONES_ROWS = 16
VT_ROWS = HEAD_DIM + ONES_ROWS
INT_MAX = 2 ** 31 - 1
INT_MIN = -(2 ** 31)
KEY_POS_INF = 0x7F800000
KEY_NEG_INF = -KEY_POS_INF
PACK16 = 16
FLT_LOWEST = -3.4028234663852886e38
MAX_BISECT = 18
BISECT_UNCHECKED = 12
HGRN_ROWS = 512
TILE_UNROLL = 4
BISECT_PER_CHECK = 2

f32 = jnp.float32
bf16 = jnp.bfloat16
i32 = jnp.int32
i16 = jnp.int16
coarse = jnp.bfloat16


def _sigmoid(v):
    return 1.0 / (1.0 + jnp.exp(-v))


def _normed_rows(x_ref, g_ref):
    x = x_ref[...]
    ms = jnp.mean(x * x, axis=-1, keepdims=True)
    return ((x * lax.rsqrt(ms + EPS)) * g_ref[...]).astype(bf16)


def _rope_fn(cos1, sin1, tm):
    cos = jnp.concatenate([cos1] * 4, axis=-1)
    sin = jnp.concatenate([sin1] * 4, axis=-1)
    lane = lax.broadcasted_iota(i32, (tm, ATTN_WIDTH), 1)
    low = (lane & (HEAD_DIM - 1)) < ROPE_HALF

    def rope(v):
        partner = jnp.where(low, pltpu.roll(v, ATTN_WIDTH - ROPE_HALF, 1),
                            pltpu.roll(v, ROPE_HALF, 1))
        return v * cos + partner * sin
    return rope


def _hgrn_gates(lbl_ref, hf):
    lg = lbl_ref[...]
    e = jnp.exp(lg - jnp.max(lg, axis=0, keepdims=True))
    lb = e[0:1, :] / jnp.sum(e, axis=0, keepdims=True)
    f = lb + (1.0 - lb) * _sigmoid(hf)
    return jnp.log(f), 1.0 - f


def _inproj_kernel(x_ref, g_ref, w_ref, cos_ref, sin_ref, lng_ref, lnb_ref, lbl_ref,
                   qat_ref, ka_ref, vt_ref, ga_ref, qit_ref, ki_ref, wt_ref,
                   hq_ref, hlf_ref, hk_ref, hv_ref, hg_ref):
    xb = _normed_rows(x_ref, g_ref)
    tm = xb.shape[0]
    cos1 = cos_ref[...]
    sin1 = sin_ref[...]
    rope = _rope_fn(cos1, sin1, tm)

    def proj(g):
        return jnp.dot(xb, w_ref[:, g * 512:(g + 1) * 512], preferred_element_type=f32)

    qat_ref[...] = (rope(proj(0)) * (ATTN_SCALE * LOG2E)).T.astype(bf16)
    ka_ref[...] = rope(proj(1)).astype(bf16)
    vt = proj(2).T
    ones = jnp.ones((ONES_ROWS, tm), f32)
    parts = []
    for h in range(ATTN_HEADS):
        parts += [vt[h * HEAD_DIM:(h + 1) * HEAD_DIM], ones]
    vt_ref[0] = jnp.concatenate(parts, axis=0).astype(bf16)
    ag = proj(3)
    ga_ref[...] = (ag * _sigmoid(ag)).astype(bf16)
    qit_ref[...] = rope(proj(4)).T.astype(bf16)

    hq = proj(5)
    hq_ref[...] = hq * _sigmoid(hq)
    hlf_ref[...], hk_ref[...] = _hgrn_gates(lbl_ref, proj(6))
    hv_ref[...] = proj(7)
    hg = proj(8)
    hg_ref[...] = (hg * _sigmoid(hg)).astype(bf16)

    sm = jnp.dot(xb, w_ref[:, N_WIDE * 512:N_WIDE * 512 + SMALL_W], preferred_element_type=f32)
    lane1 = lax.broadcasted_iota(i32, (tm, SMALL_W), 1)
    is_key = lane1 < IDX_DIM
    mu = jnp.sum(jnp.where(is_key, sm, 0.0), axis=-1, keepdims=True) / IDX_DIM
    d = sm - mu
    var = jnp.sum(jnp.where(is_key, d * d, 0.0), axis=-1, keepdims=True) / IDX_DIM
    y = d * lax.rsqrt(var + EPS) * lng_ref[...] + lnb_ref[...]
    low1 = (lane1 & (IDX_DIM - 1)) < ROPE_HALF
    partner = jnp.where(low1, pltpu.roll(y, SMALL_W - ROPE_HALF, 1), pltpu.roll(y, ROPE_HALF, 1))
    y = y * cos1 + partner * sin1
    ki_ref[...] = y[:, :IDX_DIM].astype(bf16)
    wt_ref[...] = sm.T[IDX_DIM:IDX_DIM + IDX_HEADS] * IDX_SCALE


def _inproj(x2, g, w, cos_t, sin_t, lng, lnb, lbl, n_pos_blocks):
    n, d_model = x2.shape
    tm = TILE
    row = lambda i: (i, 0)
    col = lambda i: (0, i)
    const = lambda i: (0, 0)
    pos = lambda i: (i % n_pos_blocks, 0)
    wide_bf = jax.ShapeDtypeStruct((n, 512), bf16)
    wide_f = jax.ShapeDtypeStruct((n, 512), f32)
    wide_t = jax.ShapeDtypeStruct((512, n), bf16)
    out_shape = (wide_t, wide_bf,
                 jax.ShapeDtypeStruct((n // tm, ATTN_HEADS * VT_ROWS, tm), bf16),
                 wide_bf, wide_t,
                 jax.ShapeDtypeStruct((n, IDX_DIM), bf16),
                 jax.ShapeDtypeStruct((IDX_HEADS, n), f32),
                 wide_f, wide_f, wide_f, wide_f, wide_bf)
    wide_spec = pl.BlockSpec((tm, 512), row)
    t_spec = pl.BlockSpec((512, tm), col)
    out_specs = (t_spec, wide_spec,
                 pl.BlockSpec((1, ATTN_HEADS * VT_ROWS, tm), lambda i: (i, 0, 0)),
                 wide_spec, t_spec,
                 pl.BlockSpec((tm, IDX_DIM), row),
                 pl.BlockSpec((IDX_HEADS, tm), col)) + (wide_spec,) * 5
    return pl.pallas_call(
        _inproj_kernel,
        grid=(n // tm,),
        in_specs=[
            pl.BlockSpec((tm, d_model), row),
            pl.BlockSpec((1, d_model), const),
            pl.BlockSpec(w.shape, const),
            pl.BlockSpec((tm, LANES), pos),
            pl.BlockSpec((tm, LANES), pos),
            pl.BlockSpec((1, SMALL_W), const),
            pl.BlockSpec((1, SMALL_W), const),
            pl.BlockSpec(lbl.shape, const),
        ],
        out_specs=out_specs,
        out_shape=out_shape,
        compiler_params=pltpu.CompilerParams(
            dimension_semantics=("parallel",), vmem_limit_bytes=VMEM_LIMIT),
        name="inproj",
    )(x2, g, w, cos_t, sin_t, lng, lnb, lbl)


def _meta_proj_kernel(x_ref, g_ref, w_ref, cos_ref, sin_ref, lbl_ref,
                      ka_ref, va_ref, hlf_ref, hk_ref, hv_ref):
    xb = _normed_rows(x_ref, g_ref)
    rope = _rope_fn(cos_ref[...], sin_ref[...], xb.shape[0])

    def proj(g):
        return jnp.dot(xb, w_ref[:, g * 512:(g + 1) * 512], preferred_element_type=f32)

    ka_ref[...] = rope(proj(1)).astype(bf16)
    va_ref[...] = proj(2).astype(bf16)
    hlf_ref[...], hk_ref[...] = _hgrn_gates(lbl_ref, proj(6))
    hv_ref[...] = proj(7)


def _meta_proj(meta, g, w, cos_t, sin_t, lbl):
    n, d_model = meta.shape
    full = lambda a: pl.BlockSpec(a.shape, lambda i: (0,) * a.ndim)
    wide = pl.BlockSpec((n, 512), lambda i: (0, 0))
    return pl.pallas_call(
        _meta_proj_kernel,
        grid=(1,),
        in_specs=[full(meta), full(g), full(w), full(cos_t), full(sin_t), full(lbl)],
        out_specs=(wide,) * 5,
        out_shape=(jax.ShapeDtypeStruct((n, 512), bf16),) * 2
        + (jax.ShapeDtypeStruct((n, 512), f32),) * 3,
        compiler_params=pltpu.CompilerParams(vmem_limit_bytes=VMEM_LIMIT),
        name="meta_proj",
    )(meta, g, w, cos_t, sin_t, lbl)


def _unkey(k):
    return lax.bitcast_convert_type(jnp.where(k < 0, INT_MIN - k, k), f32)


def _for_each_tile(n, body):
    def trip(j, carry):
        for u in range(TILE_UNROLL):
            body(TILE_UNROLL * j + u)
        return carry

    lax.fori_loop(0, n // TILE_UNROLL, trip, 0)
    group = TILE_UNROLL // 2
    while group >= 1:
        def rest(group=group):
            start = (n // (2 * group)) * (2 * group)
            for u in range(group):
                body(start + u)
        pl.when((n & group) != 0)(rest)
        group //= 2


def _sum_tiles(n, tile_fn, acc_ref):
    z = jnp.zeros(acc_ref.shape, acc_ref.dtype)

    def trip(j, accs):
        return tile_fn(2 * j + 1, tile_fn(2 * j, accs))

    a = lax.fori_loop(0, n // 2, trip, (z, z, z, z))
    acc_ref[...] = (a[0] + a[1]) + (a[2] + a[3])

    @pl.when((n & 1) != 0)
    def _():
        b = tile_fn(n - 1, (acc_ref[...], z, z, z))
        acc_ref[...] = (b[0] + b[1]) + (b[2] + b[3])

    return acc_ref[...]


def _attn_kernel(qat_ref, qit_ref, wt_ref, ki_ref, ka_ref, vt_ref, km_ref, vm_ref, o_ref,
                 sc_ref, sb_ref, s_ref, acc_ref, m_ref, thr_ref, cnt_ref, cnt16_ref, *, t, topk):
    i = pl.program_id(1)
    nkb = i + 1
    key_l = lax.broadcasted_iota(i32, (t, t), 0)
    qry_l = lax.broadcasted_iota(i32, (t, t), 1)

    def score_tile(kb, causal):
        kik = ki_ref[pl.ds(pl.multiple_of(kb * t, t), t), :]
        acc = jnp.zeros((t, t), f32)
        for h in range(IDX_HEADS):
            s = jnp.dot(kik, qit_ref[h * IDX_DIM:(h + 1) * IDX_DIM, :], preferred_element_type=f32)
            acc = acc + jnp.maximum(s, 0.0) * wt_ref[h:h + 1, :]
        if causal:
            acc = jnp.where(key_l <= qry_l, acc, -jnp.inf)
        sc_ref[kb] = acc
        sb_ref[kb] = acc.astype(coarse)

    _for_each_tile(i, lambda kb: score_tile(kb, False))
    score_tile(i, True)

    sub_l = lax.broadcasted_iota(i32, (SUBLANES, t), 0)

    def count(pred):
        def tile(kb, accs):
            accs = list(accs)
            for r in range(t // SUBLANES):
                v = sc_ref[kb, r * SUBLANES:(r + 1) * SUBLANES, :]
                accs[r % 4] = accs[r % 4] + jnp.where(pred(v, kb * t + r * SUBLANES + sub_l), 1, 0)
            return tuple(accs)
        return jnp.sum(_sum_tiles(nkb, tile, cnt_ref), axis=0, keepdims=True)

    def count_ge(cand):
        cb = jnp.broadcast_to(cand, (SUBLANES, t))
        return count(lambda v, idx: v >= cb)

    def count_ge_coarse(cand):
        cb = jnp.broadcast_to(cand, (PACK16, t)).astype(coarse)
        one, zero = jnp.int16(1), jnp.int16(0)

        def tile(kb, accs):
            accs = list(accs)
            for r in range(t // PACK16):
                v = sb_ref[kb, r * PACK16:(r + 1) * PACK16, :]
                accs[r % 4] = accs[r % 4] + jnp.where(v >= cb, one, zero)
            return tuple(accs)
        return jnp.sum(_sum_tiles(nkb, tile, cnt16_ref).astype(i32), axis=0, keepdims=True)

    def bisect(count_fn, to_float, carry):
        lo, hi, c_lo, c_hi = carry
        mid = lo + lax.shift_right_logical(hi - lo, 1)
        c = count_fn(to_float(mid))
        up = (c >= topk) & (hi != lo + 1)
        dn = (c < topk) & (hi != lo + 1)
        return (jnp.where(up, mid, lo), jnp.where(dn, mid, hi),
                jnp.where(up, c, c_lo), jnp.where(dn, c, c_hi))

    row = lambda val: jnp.full((1, t), val, i32)
    top_lo, top_hi = KEY_NEG_INF >> 16, KEY_POS_INF >> 16
    top, _, _, _ = lax.fori_loop(
        0, (top_hi - top_lo - 1).bit_length(),
        lambda j, c: bisect(count_ge_coarse, lambda m: _unkey(m * (1 << 16)), c),
        (row(top_lo), row(top_hi), row(0) + nkb * t, row(0)))

    lo0 = jnp.maximum(top - 1, top_lo) * (1 << 16)
    hi0 = jnp.minimum(top + 1, top_hi) * (1 << 16)
    c_hi0 = count_ge(_unkey(hi0))

    def cond(carry):
        it, lo, hi, c_lo, c_hi = carry
        open_ = jnp.where((c_lo != topk) & (hi != lo + 1), 1.0, 0.0)
        return (it < MAX_BISECT) & (jnp.max(open_) > 0.0)

    def step(carry):
        it, state = carry[0], carry[1:]
        for _ in range(BISECT_PER_CHECK):
            state = bisect(count_ge, _unkey, state)
        return (it + BISECT_PER_CHECK,) + state

    state = lax.fori_loop(0, BISECT_UNCHECKED, lambda j, st: bisect(count_ge, _unkey, st),
                          (lo0, hi0, row(INT_MAX), c_hi0))
    _, lo, hi, c_lo, c_hi = lax.while_loop(cond, step, (jnp.int32(BISECT_UNCHECKED),) + state)
    thr = _unkey(lo)
    thr_ref[...] = jnp.broadcast_to(jnp.maximum(thr, FLT_LOWEST), (SUBLANES, t))

    @pl.when(jnp.max(c_lo) > topk)
    def _():
        tb = jnp.broadcast_to(thr, (SUBLANES, t))
        need = topk - c_hi
        zero = jnp.zeros((1, t), i32)

        def tied_in(ref_tile, bound):
            z = jnp.zeros((SUBLANES, t), i32)
            accs = [z, z, z, z]
            for r in range(t // SUBLANES):
                v = ref_tile[r * SUBLANES:(r + 1) * SUBLANES, :]
                hit = v == tb
                if bound is not None:
                    hit = hit & (r * SUBLANES + sub_l < bound)
                accs[r % 4] = accs[r % 4] + jnp.where(hit, 1, 0)
            return jnp.sum((accs[0] + accs[1]) + (accs[2] + accs[3]), axis=0, keepdims=True)

        def locate(kb, carry):
            seen, tile_of, before = carry
            new = seen + tied_in(sc_ref.at[kb], None)
            cross = (seen < need) & (new >= need)
            return new, jnp.where(cross, kb, tile_of), jnp.where(cross, seen, before)

        _, tile_of, before = lax.fori_loop(0, nkb, locate, (zero, zero, zero))
        need_in = need - before

        g_ref = s_ref.at[0]
        g_ref[...] = sc_ref[0]

        def gather(kb, carry):
            g_ref[...] = jnp.where(tile_of == kb, sc_ref[kb], g_ref[...])
            return carry

        lax.fori_loop(1, nkb, gather, 0)

        tile_bits = (t - 1).bit_length()

        def idx_body(j, jcol):
            cand = jcol | (1 << (tile_bits - 1 - j))
            c = tied_in(g_ref, jnp.broadcast_to(cand, (SUBLANES, t)))
            return jnp.where(c < need_in, cand, jcol)

        jcol = tile_of * t + lax.fori_loop(0, tile_bits, idx_body, zero)
        jcol = jnp.where(c_lo > topk, jcol, INT_MAX)

        def drop_body(kb, carry):
            v = sc_ref[kb]
            sc_ref[kb] = jnp.where((v == thr) & (kb * t + key_l > jcol), -jnp.inf, v)
            return carry

        lax.fori_loop(0, nkb, drop_body, 0)


    row_h = lax.broadcasted_iota(i32, (2 * HEAD_DIM, t), 0)
    first = row_h < HEAD_DIM

    def q_head(h):
        qp = qat_ref[(h // 2) * LANES:(h // 2 + 1) * LANES, :]
        return jnp.where(first if h % 2 == 0 else ~first, qp, jnp.zeros_like(qp))

    for h in range(ATTN_HEADS):
        cols = slice((h // 2) * LANES, (h // 2 + 1) * LANES)
        s = jnp.dot(km_ref[:, cols], q_head(h), preferred_element_type=f32)
        m0 = jnp.max(s, axis=0, keepdims=True)
        p = jnp.exp2(s - m0).astype(bf16)
        pv = lax.dot_general(vm_ref[:, cols], p, (((0,), (0,)), ((), ())), preferred_element_type=f32)
        pv = pv[(h % 2) * HEAD_DIM:(h % 2 + 1) * HEAD_DIM]
        l0 = jnp.sum(p.astype(f32), axis=0, keepdims=True)
        acc_ref[h] = jnp.concatenate([pv, jnp.broadcast_to(l0, (ONES_ROWS, t))], axis=0)
        m_ref[h] = jnp.broadcast_to(m0, (SUBLANES, t))

    thr_b = thr_ref[0:1, :]

    def attn_body(kb, carry):
        r0 = pl.multiple_of(kb * t, t)
        bias = jnp.where(sc_ref[kb] >= thr_b, 0.0, -jnp.inf)
        alphas, m_news = [], []
        for h in range(ATTN_HEADS):
            cols = slice((h // 2) * LANES, (h // 2 + 1) * LANES)
            s = jnp.dot(ka_ref[pl.ds(r0, t), cols], q_head(h), preferred_element_type=f32) + bias
            s_ref[h] = s
            m_old = m_ref[h][0:1, :]
            m_new = jnp.maximum(m_old, jnp.max(s, axis=0, keepdims=True))
            m_ref[h] = jnp.broadcast_to(m_new, (SUBLANES, t))
            alphas.append(jnp.exp2(m_old - m_new))
            m_news.append(m_new)
        for h in range(ATTN_HEADS):
            p = jnp.exp2(s_ref[h] - m_news[h]).astype(bf16)
            pv = jnp.dot(vt_ref[kb, h * VT_ROWS:(h + 1) * VT_ROWS, :], p, preferred_element_type=f32)
            acc_ref[h] = alphas[h] * acc_ref[h] + pv
        return carry

    _for_each_tile(nkb, lambda kb: attn_body(kb, 0))

    outs = []
    for h in range(ATTN_HEADS):
        a = acc_ref[h]
        outs.append(a[:HEAD_DIM] / a[HEAD_DIM:HEAD_DIM + 1])
    o_ref[...] = jnp.concatenate(outs, axis=0).T.astype(bf16)


def _sparse_attention(qat, qit, wt, ki, ka, vt, km, vm, batch, seq):
    t = TILE
    n = ka.shape[0]
    nq = seq // t
    topk = min(TOPK_MAX, seq // 4)
    qcol = lambda b, i: (0, b * nq + i)
    brow = lambda b, i: (b, 0)
    const = lambda b, i: (0, 0)
    once = pl.Buffered(1)
    return pl.pallas_call(
        functools.partial(_attn_kernel, t=t, topk=topk),
        grid=(batch, nq),
        in_specs=[
            pl.BlockSpec((ATTN_WIDTH, t), qcol),
            pl.BlockSpec((ATTN_WIDTH, t), qcol),
            pl.BlockSpec((IDX_HEADS, t), qcol),
            pl.BlockSpec((seq, IDX_DIM), brow, pipeline_mode=once),
            pl.BlockSpec((seq, ATTN_WIDTH), brow, pipeline_mode=once),
            pl.BlockSpec((nq, ATTN_HEADS * VT_ROWS, t), lambda b, i: (b, 0, 0), pipeline_mode=once),
            pl.BlockSpec((N_META, ATTN_WIDTH), const),
            pl.BlockSpec((N_META, ATTN_WIDTH), const),
        ],
        out_specs=pl.BlockSpec((t, ATTN_WIDTH), lambda b, i: (b * nq + i, 0)),
        out_shape=jax.ShapeDtypeStruct((n, ATTN_WIDTH), bf16),
        scratch_shapes=[
            pltpu.VMEM((nq, t, t), f32),
            pltpu.VMEM((nq, t, t), coarse),
            pltpu.VMEM((ATTN_HEADS, t, t), f32),
            pltpu.VMEM((ATTN_HEADS, VT_ROWS, t), f32),
            pltpu.VMEM((ATTN_HEADS, SUBLANES, t), f32),
            pltpu.VMEM((SUBLANES, t), f32),
            pltpu.VMEM((SUBLANES, t), i32),
            pltpu.VMEM((PACK16, t), i16),
        ],
        compiler_params=pltpu.CompilerParams(
            dimension_semantics=("parallel", "arbitrary"), vmem_limit_bytes=VMEM_LIMIT),
        name="sparse_attn",
    )(qat, qit, wt, ki, ka, vt, km, vm)


def _hgrn_chunk(q, lf, k, v, st, tri, band_masks, blk_lower):
    b = jnp.dot(tri, lf, preferred_element_type=f32, precision=lax.Precision.HIGHEST) * LOG2E
    bl = b[CHUNK - 1:CHUNK, :]
    q_in = (q * jnp.exp2(b)).astype(bf16)
    o = lax.dot_general(q_in, st.astype(bf16), (((1,), (1,)), ((), ())), preferred_element_type=f32)

    blocks = [jnp.zeros((SUB, CHUNK), f32)]
    for si in range(1, CHUNK // SUB):
        bref = b[si * SUB - 1:si * SUB, :]
        rows = slice(si * SUB, (si + 1) * SUB)
        q_s = (q[rows] * jnp.exp2(b[rows] - bref)).astype(bf16)
        k_s = (k * jnp.exp2(jnp.minimum(bref - b, 0.0))).astype(bf16)
        blocks.append(lax.dot_general(q_s, k_s, (((1,), (1,)), ((), ())), preferred_element_type=f32))
    a = jnp.where(blk_lower, jnp.concatenate(blocks, axis=0), 0.0)

    for dlt in range(SUB):
        if dlt == 0:
            w = q * k
        else:
            decay = jnp.exp2(jnp.minimum(b - pltpu.roll(b, dlt, 0), 0.0))
            w = q * pltpu.roll(k, dlt, 0) * decay
        a = jnp.where(band_masks[dlt], jnp.sum(w, axis=1, keepdims=True), a)

    o = o + jnp.dot(a.astype(bf16), v.astype(bf16), preferred_element_type=f32)
    k_end = (k * jnp.exp2(bl - b)).astype(bf16)
    upd = lax.dot_general(v.astype(bf16), k_end, (((0,), (0,)), ((), ())), preferred_element_type=f32)
    return o, st * jnp.exp2(bl) + upd


def _hgrn_kernel(q_ref, lf_ref, k_ref, v_ref, lfm_ref, km_ref, vm_ref, o_ref, st_ref, *, n_chunks):
    c = pl.program_id(1)
    r_i = lax.broadcasted_iota(i32, (CHUNK, CHUNK), 0)
    c_i = lax.broadcasted_iota(i32, (CHUNK, CHUNK), 1)
    tri = jnp.where(c_i <= r_i, 1.0, 0.0).astype(f32)
    sub_shift = SUB.bit_length() - 1
    blk_lower = (c_i >> sub_shift) < (r_i >> sub_shift)
    band_masks = [((r_i & (SUB - 1)) >= d) & (c_i == r_i - d) for d in range(SUB)]

    @pl.when(c == 0)
    def _():
        pad = jnp.zeros((CHUNK - N_META, HGRN_DIM), f32)
        for h in range(HGRN_HEADS):
            cols = slice(h * HGRN_DIM, (h + 1) * HGRN_DIM)
            lf = jnp.concatenate([pad, lfm_ref[:, cols]], axis=0)
            k = jnp.concatenate([pad, km_ref[:, cols]], axis=0)
            v = jnp.concatenate([pad, vm_ref[:, cols]], axis=0)
            _, st = _hgrn_chunk(jnp.zeros((CHUNK, HGRN_DIM), f32), lf, k, v,
                                jnp.zeros((HGRN_DIM, HGRN_DIM), f32), tri, band_masks, blk_lower)
            st_ref[h] = st

    def chunk_body(ci, carry):
        rows = pl.ds(pl.multiple_of(ci * CHUNK, CHUNK), CHUNK)
        for h in range(HGRN_HEADS):
            cols = slice(h * HGRN_DIM, (h + 1) * HGRN_DIM)
            o, st = _hgrn_chunk(q_ref[rows, cols], lf_ref[rows, cols], k_ref[rows, cols],
                                v_ref[rows, cols], st_ref[h], tri, band_masks, blk_lower)
            o_ref[rows, cols] = o.astype(bf16)
            st_ref[h] = st
        return carry

    lax.fori_loop(0, n_chunks, chunk_body, 0, unroll=4)


def _hgrn(hq, hlf, hk, hv, lfm, km, vm, batch, seq, tc):
    n = hq.shape[0]
    nc = seq // tc
    row = lambda b, c: (b * nc + c, 0)
    const = lambda b, c: (0, 0)
    spec = pl.BlockSpec((tc, HGRN_WIDTH), row)
    mspec = pl.BlockSpec((N_META, HGRN_WIDTH), const)
    return pl.pallas_call(
        functools.partial(_hgrn_kernel, n_chunks=tc // CHUNK),
        grid=(batch, nc),
        in_specs=[spec, spec, spec, spec, mspec, mspec, mspec],
        out_specs=spec,
        out_shape=jax.ShapeDtypeStruct((n, HGRN_WIDTH), bf16),
        scratch_shapes=[pltpu.VMEM((HGRN_HEADS, HGRN_DIM, HGRN_DIM), f32)],
        compiler_params=pltpu.CompilerParams(
            dimension_semantics=("parallel", "arbitrary"), vmem_limit_bytes=VMEM_LIMIT),
        name="hgrn2",
    )(hq, hlf, hk, hv, lfm, km, vm)


def _outproj_kernel(oa_ref, ga_ref, oh_ref, hg_ref, x_ref, w_ref, na_ref, nh_ref, nf_ref, y_ref):
    oa = oa_ref[...].astype(f32)
    a = (oa * lax.rsqrt(jnp.mean(oa * oa, axis=-1, keepdims=True) + EPS) * na_ref[...]
         * ga_ref[...].astype(f32))
    parts = []
    for h in range(HGRN_HEADS):
        cols = slice(h * HGRN_DIM, (h + 1) * HGRN_DIM)
        oh = oh_ref[:, cols].astype(f32)
        nrm = oh * lax.rsqrt(jnp.mean(oh * oh, axis=-1, keepdims=True) + EPS) * nh_ref[...]
        parts.append(nrm * hg_ref[:, cols].astype(f32))
    hh = jnp.concatenate(parts, axis=-1)
    mix = (jnp.dot(a.astype(bf16), w_ref[:ATTN_WIDTH, :], preferred_element_type=f32)
           + jnp.dot(hh.astype(bf16), w_ref[ATTN_WIDTH:, :], preferred_element_type=f32))
    hres = x_ref[...] + mix
    y_ref[...] = hres * lax.rsqrt(jnp.mean(hres * hres, axis=-1, keepdims=True) + EPS) * nf_ref[...]


def _outproj(oa, ga, oh, hg, x2, w, na, nh, nf, tm):
    n, d_model = x2.shape
    row = lambda i: (i, 0)
    const = lambda i: (0, 0)
    half = pl.BlockSpec((tm, 512), row)
    return pl.pallas_call(
        _outproj_kernel,
        grid=(n // tm,),
        in_specs=[half, half, half, half,
                  pl.BlockSpec((tm, d_model), row),
                  pl.BlockSpec(w.shape, const),
                  pl.BlockSpec((1, ATTN_WIDTH), const),
                  pl.BlockSpec((1, HGRN_DIM), const),
                  pl.BlockSpec((1, d_model), const)],
        out_specs=pl.BlockSpec((tm, d_model), row),
        out_shape=jax.ShapeDtypeStruct((n, d_model), f32),
        compiler_params=pltpu.CompilerParams(
            dimension_semantics=("parallel",), vmem_limit_bytes=VMEM_LIMIT),
        name="outproj",
    )(oa, ga, oh, hg, x2, w, na, nh, nf)


def _rope_tables(n_pos):
    inv = ROPE_THETA ** (-jnp.arange(ROPE_HALF, dtype=f32) * 2.0 / ROPE_DIMS)
    ang = jnp.arange(n_pos, dtype=f32)[:, None] * inv[None, :]
    cos, sin = jnp.cos(ang), jnp.sin(ang)
    rest = HEAD_DIM - ROPE_DIMS
    cos_h = jnp.concatenate([cos, cos, jnp.ones((n_pos, rest), f32)], axis=-1)
    sin_h = jnp.concatenate([-sin, sin, jnp.zeros((n_pos, rest), f32)], axis=-1)
    return jnp.tile(cos_h, (1, LANES // HEAD_DIM)), jnp.tile(sin_h, (1, LANES // HEAD_DIM))


def kernel(x, meta_tokens, mix_norm, w_in, idx_k_norm_g, idx_k_norm_b, attn_out_norm,
           hgrn_lb_logits, hgrn_out_norm, w_out, final_norm):
    batch, seq, d_model = x.shape
    assert seq % TILE == 0
    assert seq % HGRN_ROWS == 0
    x2 = x.reshape(batch * seq, d_model)

    wl = w_in[0]
    a_end = 4 * ATTN_WIDTH
    iq_end = a_end + IDX_HEADS * IDX_DIM
    sm_end = iq_end + IDX_DIM + IDX_HEADS
    w_small = jnp.pad(wl[:, iq_end:sm_end], ((0, 0), (0, SMALL_W - (sm_end - iq_end))))
    w = jnp.concatenate([wl[:, :iq_end], wl[:, sm_end:], w_small], axis=-1).astype(bf16)

    cos_t, sin_t = _rope_tables(N_META + seq)
    g = mix_norm[0][None, :]
    lng = jnp.pad(idx_k_norm_g[0], (0, SMALL_W - IDX_DIM))[None, :]
    lnb = jnp.pad(idx_k_norm_b[0], (0, SMALL_W - IDX_DIM))[None, :]

    qat, ka, vt, ga, qit, ki, wt, hq, hlf, hk, hv, hg = _inproj(
        x2, g, w, cos_t[N_META:], sin_t[N_META:], lng, lnb, hgrn_lb_logits, seq // TILE)
    ka_m, va_m, hlf_m, hk_m, hv_m = _meta_proj(
        meta_tokens, g, w, cos_t[:N_META], sin_t[:N_META], hgrn_lb_logits)

    oa = _sparse_attention(qat, qit, wt, ki, ka, vt, ka_m, va_m, batch, seq)
    oh = _hgrn(hq, hlf, hk, hv, hlf_m, hk_m, hv_m, batch, seq, tc=HGRN_ROWS)
    y = _outproj(oa, ga, oh, hg, x2, w_out[0].astype(bf16), attn_out_norm[0][None, :],
                 hgrn_out_norm[0][None, :], final_norm[None, :], tm=256)
    return y.reshape(batch, seq, d_model)
```

```python
import functools
import math

import jax
import jax.numpy as jnp
from jax import lax
from jax.experimental import pallas as pl
from jax.experimental.pallas import tpu as pltpu

N_META = 16
ATTN_HEADS = 8
HEAD_DIM = 64
ATTN_WIDTH = ATTN_HEADS * HEAD_DIM
IDX_HEADS = 8
IDX_DIM = 64
IDX_SCALE = (IDX_HEADS ** -0.5) * (IDX_DIM ** -0.5)
ATTN_SCALE = HEAD_DIM ** -0.5
LOG2E = math.log2(math.e)
TOPK_MAX = 256
HGRN_HEADS = 4
HGRN_DIM = 128
HGRN_WIDTH = HGRN_HEADS * HGRN_DIM
CHUNK = 64
SUB = 16
ROPE_THETA = 500000.0
ROPE_DIMS = HEAD_DIM // 4
ROPE_HALF = ROPE_DIMS // 2
EPS = 1e-6

LANES = 128
SUBLANES = 8
VMEM_LIMIT = 56 * 1024 * 1024

TILE = 256
N_WIDE = 9
SMALL_W = 128
ONES_ROWS = 16
VT_ROWS = HEAD_DIM + ONES_ROWS
INT_MAX = 2 ** 31 - 1
INT_MIN = -(2 ** 31)
KEY_POS_INF = 0x7F800000
KEY_NEG_INF = -KEY_POS_INF
PACK16 = 16
FLT_LOWEST = -3.4028234663852886e38
MAX_BISECT = 18
BISECT_UNCHECKED = 12
HGRN_ROWS = 512
TILE_UNROLL = 4
BISECT_PER_CHECK = 2

f32 = jnp.float32
bf16 = jnp.bfloat16
i32 = jnp.int32
i16 = jnp.int16
coarse = jnp.bfloat16


def _sigmoid(v):
    return 1.0 / (1.0 + jnp.exp(-v))


def _normed_rows(x_ref, g_ref):
    x = x_ref[...]
    ms = jnp.mean(x * x, axis=-1, keepdims=True)
    return ((x * lax.rsqrt(ms + EPS)) * g_ref[...]).astype(bf16)


def _rope_fn(cos1, sin1, tm):
    cos = jnp.concatenate([cos1] * 4, axis=-1)
    sin = jnp.concatenate([sin1] * 4, axis=-1)
    lane = lax.broadcasted_iota(i32, (tm, ATTN_WIDTH), 1)
    low = (lane & (HEAD_DIM - 1)) < ROPE_HALF

    def rope(v):
        partner = jnp.where(low, pltpu.roll(v, ATTN_WIDTH - ROPE_HALF, 1),
                            pltpu.roll(v, ROPE_HALF, 1))
        return v * cos + partner * sin
    return rope


def _hgrn_gates(lbl_ref, hf):
    lg = lbl_ref[...]
    e = jnp.exp(lg - jnp.max(lg, axis=0, keepdims=True))
    lb = e[0:1, :] / jnp.sum(e, axis=0, keepdims=True)
    f = lb + (1.0 - lb) * _sigmoid(hf)
    return jnp.log(f), 1.0 - f


def _inproj_kernel(x_ref, g_ref, w_ref, cos_ref, sin_ref, lng_ref, lnb_ref, lbl_ref,
                   qat_ref, ka_ref, vt_ref, ga_ref, qit_ref, ki_ref, wt_ref,
                   hq_ref, hlf_ref, hk_ref, hv_ref, hg_ref):
    xb = _normed_rows(x_ref, g_ref)
    tm = xb.shape[0]
    cos1 = cos_ref[...]
    sin1 = sin_ref[...]
    rope = _rope_fn(cos1, sin1, tm)

    def proj(g):
        return jnp.dot(xb, w_ref[:, g * 512:(g + 1) * 512], preferred_element_type=f32)

    qat_ref[...] = (rope(proj(0)) * (ATTN_SCALE * LOG2E)).T.astype(bf16)
    ka_ref[...] = rope(proj(1)).astype(bf16)
    vt = proj(2).T
    ones = jnp.ones((ONES_ROWS, tm), f32)
    parts = []
    for h in range(ATTN_HEADS):
        parts += [vt[h * HEAD_DIM:(h + 1) * HEAD_DIM], ones]
    vt_ref[0] = jnp.concatenate(parts, axis=0).astype(bf16)
    ag = proj(3)
    ga_ref[...] = (ag * _sigmoid(ag)).astype(bf16)
    qit_ref[...] = rope(proj(4)).T.astype(bf16)

    hq = proj(5)
    hq_ref[...] = hq * _sigmoid(hq)
    hlf_ref[...], hk_ref[...] = _hgrn_gates(lbl_ref, proj(6))
    hv_ref[...] = proj(7)
    hg = proj(8)
    hg_ref[...] = (hg * _sigmoid(hg)).astype(bf16)

    sm = jnp.dot(xb, w_ref[:, N_WIDE * 512:N_WIDE * 512 + SMALL_W], preferred_element_type=f32)
    lane1 = lax.broadcasted_iota(i32, (tm, SMALL_W), 1)
    is_key = lane1 < IDX_DIM
    mu = jnp.sum(jnp.where(is_key, sm, 0.0), axis=-1, keepdims=True) / IDX_DIM
    d = sm - mu
    var = jnp.sum(jnp.where(is_key, d * d, 0.0), axis=-1, keepdims=True) / IDX_DIM
    y = d * lax.rsqrt(var + EPS) * lng_ref[...] + lnb_ref[...]
    low1 = (lane1 & (IDX_DIM - 1)) < ROPE_HALF
    partner = jnp.where(low1, pltpu.roll(y, SMALL_W - ROPE_HALF, 1), pltpu.roll(y, ROPE_HALF, 1))
    y = y * cos1 + partner * sin1
    ki_ref[...] = y[:, :IDX_DIM].astype(bf16)
    wt_ref[...] = sm.T[IDX_DIM:IDX_DIM + IDX_HEADS] * IDX_SCALE


def _inproj(x2, g, w, cos_t, sin_t, lng, lnb, lbl, n_pos_blocks):
    n, d_model = x2.shape
    tm = TILE
    row = lambda i: (i, 0)
    col = lambda i: (0, i)
    const = lambda i: (0, 0)
    pos = lambda i: (i % n_pos_blocks, 0)
    wide_bf = jax.ShapeDtypeStruct((n, 512), bf16)
    wide_f = jax.ShapeDtypeStruct((n, 512), f32)
    wide_t = jax.ShapeDtypeStruct((512, n), bf16)
    out_shape = (wide_t, wide_bf,
                 jax.ShapeDtypeStruct((n // tm, ATTN_HEADS * VT_ROWS, tm), bf16),
                 wide_bf, wide_t,
                 jax.ShapeDtypeStruct((n, IDX_DIM), bf16),
                 jax.ShapeDtypeStruct((IDX_HEADS, n), f32),
                 wide_f, wide_f, wide_f, wide_f, wide_bf)
    wide_spec = pl.BlockSpec((tm, 512), row)
    t_spec = pl.BlockSpec((512, tm), col)
    out_specs = (t_spec, wide_spec,
                 pl.BlockSpec((1, ATTN_HEADS * VT_ROWS, tm), lambda i: (i, 0, 0)),
                 wide_spec, t_spec,
                 pl.BlockSpec((tm, IDX_DIM), row),
                 pl.BlockSpec((IDX_HEADS, tm), col)) + (wide_spec,) * 5
    return pl.pallas_call(
        _inproj_kernel,
        grid=(n // tm,),
        in_specs=[
            pl.BlockSpec((tm, d_model), row),
            pl.BlockSpec((1, d_model), const),
            pl.BlockSpec(w.shape, const),
            pl.BlockSpec((tm, LANES), pos),
            pl.BlockSpec((tm, LANES), pos),
            pl.BlockSpec((1, SMALL_W), const),
            pl.BlockSpec((1, SMALL_W), const),
            pl.BlockSpec(lbl.shape, const),
        ],
        out_specs=out_specs,
        out_shape=out_shape,
        compiler_params=pltpu.CompilerParams(
            dimension_semantics=("parallel",), vmem_limit_bytes=VMEM_LIMIT),
        name="inproj",
    )(x2, g, w, cos_t, sin_t, lng, lnb, lbl)


def _meta_proj_kernel(x_ref, g_ref, w_ref, cos_ref, sin_ref, lbl_ref,
                      ka_ref, va_ref, hlf_ref, hk_ref, hv_ref):
    xb = _normed_rows(x_ref, g_ref)
    rope = _rope_fn(cos_ref[...], sin_ref[...], xb.shape[0])

    def proj(g):
        return jnp.dot(xb, w_ref[:, g * 512:(g + 1) * 512], preferred_element_type=f32)

    ka_ref[...] = rope(proj(1)).astype(bf16)
    va_ref[...] = proj(2).astype(bf16)
    hlf_ref[...], hk_ref[...] = _hgrn_gates(lbl_ref, proj(6))
    hv_ref[...] = proj(7)


def _meta_proj(meta, g, w, cos_t, sin_t, lbl):
    n, d_model = meta.shape
    full = lambda a: pl.BlockSpec(a.shape, lambda i: (0,) * a.ndim)
    wide = pl.BlockSpec((n, 512), lambda i: (0, 0))
    return pl.pallas_call(
        _meta_proj_kernel,
        grid=(1,),
        in_specs=[full(meta), full(g), full(w), full(cos_t), full(sin_t), full(lbl)],
        out_specs=(wide,) * 5,
        out_shape=(jax.ShapeDtypeStruct((n, 512), bf16),) * 2
        + (jax.ShapeDtypeStruct((n, 512), f32),) * 3,
        compiler_params=pltpu.CompilerParams(vmem_limit_bytes=VMEM_LIMIT),
        name="meta_proj",
    )(meta, g, w, cos_t, sin_t, lbl)


def _unkey(k):
    return lax.bitcast_convert_type(jnp.where(k < 0, INT_MIN - k, k), f32)


def _for_each_tile(n, body):
    def trip(j, carry):
        for u in range(TILE_UNROLL):
            body(TILE_UNROLL * j + u)
        return carry

    lax.fori_loop(0, n // TILE_UNROLL, trip, 0)
    group = TILE_UNROLL // 2
    while group >= 1:
        def rest(group=group):
            start = (n // (2 * group)) * (2 * group)
            for u in range(group):
                body(start + u)
        pl.when((n & group) != 0)(rest)
        group //= 2


def _sum_tiles(n, tile_fn, acc_ref):
    z = jnp.zeros(acc_ref.shape, acc_ref.dtype)

    def trip(j, accs):
        return tile_fn(2 * j + 1, tile_fn(2 * j, accs))

    a = lax.fori_loop(0, n // 2, trip, (z, z, z, z))
    acc_ref[...] = (a[0] + a[1]) + (a[2] + a[3])

    @pl.when((n & 1) != 0)
    def _():
        b = tile_fn(n - 1, (acc_ref[...], z, z, z))
        acc_ref[...] = (b[0] + b[1]) + (b[2] + b[3])

    return acc_ref[...]


def _attn_kernel(qat_ref, qit_ref, wt_ref, ki_ref, ka_ref, vt_ref, km_ref, vm_ref, o_ref,
                 sc_ref, sb_ref, s_ref, qm_ref, acc_ref, m_ref, thr_ref, cnt_ref, cnt16_ref,
                 *, t, topk):
    i = pl.program_id(1)
    nkb = i + 1
    key_l = lax.broadcasted_iota(i32, (t, t), 0)
    qry_l = lax.broadcasted_iota(i32, (t, t), 1)

    def score_tile(kb, causal):
        kik = ki_ref[pl.ds(pl.multiple_of(kb * t, t), t), :]
        acc = jnp.zeros((t, t), f32)
        for h in range(IDX_HEADS):
            s = jnp.dot(kik, qit_ref[h * IDX_DIM:(h + 1) * IDX_DIM, :], preferred_element_type=f32)
            acc = acc + jnp.maximum(s, 0.0) * wt_ref[h:h + 1, :]
        if causal:
            acc = jnp.where(key_l <= qry_l, acc, -jnp.inf)
        sc_ref[kb] = acc
        sb_ref[kb] = acc.astype(coarse)

    _for_each_tile(i, lambda kb: score_tile(kb, False))
    score_tile(i, True)

    sub_l = lax.broadcasted_iota(i32, (SUBLANES, t), 0)

    def count_ge(cand):
        cb = jnp.broadcast_to(cand, (SUBLANES, t))

        def tile(kb, accs):
            accs = list(accs)
            for r in range(t // SUBLANES):
                v = sc_ref[kb, r * SUBLANES:(r + 1) * SUBLANES, :]
                accs[r % 4] = accs[r % 4] + jnp.where(v >= cb, 1, 0)
            return tuple(accs)
        return jnp.sum(_sum_tiles(nkb, tile, cnt_ref), axis=0, keepdims=True)

    def count_ge_coarse(cand):
        cb = jnp.broadcast_to(cand, (PACK16, t)).astype(coarse)
        one, zero = jnp.int16(1), jnp.int16(0)

        def tile(kb, accs):
            accs = list(accs)
            for r in range(t // PACK16):
                v = sb_ref[kb, r * PACK16:(r + 1) * PACK16, :]
                accs[r % 4] = accs[r % 4] + jnp.where(v >= cb, one, zero)
            return tuple(accs)
        return jnp.sum(_sum_tiles(nkb, tile, cnt16_ref).astype(i32), axis=0, keepdims=True)

    def bisect(count_fn, to_float, carry, below_hi=False):
        lo, hi, c_lo, c_hi = carry
        mid = hi - 1 if below_hi else lo + lax.shift_right_logical(hi - lo, 1)
        c = count_fn(to_float(mid))
        up = (c >= topk) & (hi != lo + 1)
        dn = (c < topk) & (hi != lo + 1)
        return (jnp.where(up, mid, lo), jnp.where(dn, mid, hi),
                jnp.where(up, c, c_lo), jnp.where(dn, c, c_hi))

    row = lambda val: jnp.full((1, t), val, i32)
    top_lo, top_hi = KEY_NEG_INF >> 16, KEY_POS_INF >> 16
    top, _, _, _ = lax.fori_loop(
        0, (top_hi - top_lo - 1).bit_length(),
        lambda j, c: bisect(count_ge_coarse, lambda m: _unkey(m * (1 << 16)), c),
        (row(top_lo), row(top_hi), row(0) + nkb * t, row(0)))

    lo0 = jnp.maximum(top - 1, top_lo) * (1 << 16)
    hi0 = jnp.minimum(top + 1, top_hi) * (1 << 16)
    c_hi0 = count_ge(_unkey(hi0))

    def cond(carry):
        it, lo, hi, c_lo, c_hi = carry
        open_ = jnp.where((c_lo != topk) & (hi != lo + 1), 1.0, 0.0)
        return (it < MAX_BISECT) & (jnp.max(open_) > 0.0)

    def step(carry):
        it, state = carry[0], carry[1:]
        for _ in range(BISECT_PER_CHECK):
            state = bisect(count_ge, _unkey, state)
        return (it + BISECT_PER_CHECK,) + state

    state = bisect(count_ge, _unkey, (lo0, hi0, row(INT_MAX), c_hi0), below_hi=True)
    state = lax.fori_loop(0, BISECT_UNCHECKED, lambda j, st: bisect(count_ge, _unkey, st), state)
    _, lo, hi, c_lo, c_hi = lax.while_loop(cond, step, (jnp.int32(BISECT_UNCHECKED),) + state)
    thr = _unkey(lo)
    thr_ref[...] = jnp.broadcast_to(jnp.maximum(thr, FLT_LOWEST), (SUBLANES, t))

    @pl.when(jnp.max(c_lo) > topk)
    def _():
        tb = jnp.broadcast_to(thr, (SUBLANES, t))
        need = topk - c_hi
        zero = jnp.zeros((1, t), i32)

        def tied_in(ref_tile, bound):
            z = jnp.zeros((SUBLANES, t), i32)
            accs = [z, z, z, z]
            for r in range(t // SUBLANES):
                v = ref_tile[r * SUBLANES:(r + 1) * SUBLANES, :]
                hit = v == tb
                if bound is not None:
                    hit = hit & (r * SUBLANES + sub_l < bound)
                accs[r % 4] = accs[r % 4] + jnp.where(hit, 1, 0)
            return jnp.sum((accs[0] + accs[1]) + (accs[2] + accs[3]), axis=0, keepdims=True)

        def locate(kb, carry):
            seen, tile_of, before = carry
            new = seen + tied_in(sc_ref.at[kb], None)
            cross = (seen < need) & (new >= need)
            return new, jnp.where(cross, kb, tile_of), jnp.where(cross, seen, before)

        _, tile_of, before = lax.fori_loop(0, nkb, locate, (zero, zero, zero))
        need_in = need - before

        g_ref = s_ref.at[0]
        g_ref[...] = sc_ref[0]

        def gather(kb, carry):
            g_ref[...] = jnp.where(tile_of == kb, sc_ref[kb], g_ref[...])
            return carry

        lax.fori_loop(1, nkb, gather, 0)

        tile_bits = (t - 1).bit_length()

        def idx_body(j, jcol):
            cand = jcol | (1 << (tile_bits - 1 - j))
            c = tied_in(g_ref, jnp.broadcast_to(cand, (SUBLANES, t)))
            return jnp.where(c < need_in, cand, jcol)

        jcol = tile_of * t + lax.fori_loop(0, tile_bits, idx_body, zero)
        jcol = jnp.where(c_lo > topk, jcol, INT_MAX)

        def drop_body(kb, carry):
            v = sc_ref[kb]
            sc_ref[kb] = jnp.where((v == thr) & (kb * t + key_l > jcol), -jnp.inf, v)
            return carry

        lax.fori_loop(0, nkb, drop_body, 0)


    row_h = lax.broadcasted_iota(i32, (2 * HEAD_DIM, t), 0)
    first = row_h < HEAD_DIM

    def q_head(h):
        return qm_ref[h]

    for h in range(ATTN_HEADS):
        cols = slice((h // 2) * LANES, (h // 2 + 1) * LANES)
        qp = qat_ref[cols, :]
        qm_ref[h] = jnp.where(first if h % 2 == 0 else ~first, qp, jnp.zeros_like(qp))
        s = jnp.dot(km_ref[:, cols], q_head(h), preferred_element_type=f32)
        m0 = jnp.max(s, axis=0, keepdims=True)
        p = jnp.exp2(s - m0).astype(bf16)
        pv = lax.dot_general(vm_ref[:, cols], p, (((0,), (0,)), ((), ())), preferred_element_type=f32)
        pv = pv[(h % 2) * HEAD_DIM:(h % 2 + 1) * HEAD_DIM]
        l0 = jnp.sum(p.astype(f32), axis=0, keepdims=True)
        acc_ref[h] = jnp.concatenate([pv, jnp.broadcast_to(l0, (ONES_ROWS, t))], axis=0)
        m_ref[h] = jnp.broadcast_to(m0, (SUBLANES, t))

    thr_b = thr_ref[0:1, :]

    def attn_body(kb, carry):
        r0 = pl.multiple_of(kb * t, t)
        bias = jnp.where(sc_ref[kb] >= thr_b, 0.0, -jnp.inf)
        alphas, m_news = [], []
        for h in range(ATTN_HEADS):
            cols = slice((h // 2) * LANES, (h // 2 + 1) * LANES)
            s = jnp.dot(ka_ref[pl.ds(r0, t), cols], q_head(h), preferred_element_type=f32) + bias
            s_ref[h] = s
            m_old = m_ref[h][0:1, :]
            m_new = jnp.maximum(m_old, jnp.max(s, axis=0, keepdims=True))
            m_ref[h] = jnp.broadcast_to(m_new, (SUBLANES, t))
            alphas.append(jnp.exp2(m_old - m_new))
            m_news.append(m_new)
        for h in range(ATTN_HEADS):
            p = jnp.exp2(s_ref[h] - m_news[h]).astype(bf16)
            pv = jnp.dot(vt_ref[kb, h * VT_ROWS:(h + 1) * VT_ROWS, :], p, preferred_element_type=f32)
            acc_ref[h] = alphas[h] * acc_ref[h] + pv
        return carry

    _for_each_tile(nkb, lambda kb: attn_body(kb, 0))

    outs = []
    for h in range(ATTN_HEADS):
        a = acc_ref[h]
        outs.append(a[:HEAD_DIM] / a[HEAD_DIM:HEAD_DIM + 1])
    o_ref[...] = jnp.concatenate(outs, axis=0).T.astype(bf16)


def _sparse_attention(qat, qit, wt, ki, ka, vt, km, vm, batch, seq):
    t = TILE
    n = ka.shape[0]
    nq = seq // t
    topk = min(TOPK_MAX, seq // 4)
    qcol = lambda b, i: (0, b * nq + i)
    brow = lambda b, i: (b, 0)
    const = lambda b, i: (0, 0)
    once = pl.Buffered(1)
    return pl.pallas_call(
        functools.partial(_attn_kernel, t=t, topk=topk),
        grid=(batch, nq),
        in_specs=[
            pl.BlockSpec((ATTN_WIDTH, t), qcol),
            pl.BlockSpec((ATTN_WIDTH, t), qcol),
            pl.BlockSpec((IDX_HEADS, t), qcol),
            pl.BlockSpec((seq, IDX_DIM), brow, pipeline_mode=once),
            pl.BlockSpec((seq, ATTN_WIDTH), brow, pipeline_mode=once),
            pl.BlockSpec((nq, ATTN_HEADS * VT_ROWS, t), lambda b, i: (b, 0, 0), pipeline_mode=once),
            pl.BlockSpec((N_META, ATTN_WIDTH), const),
            pl.BlockSpec((N_META, ATTN_WIDTH), const),
        ],
        out_specs=pl.BlockSpec((t, ATTN_WIDTH), lambda b, i: (b * nq + i, 0)),
        out_shape=jax.ShapeDtypeStruct((n, ATTN_WIDTH), bf16),
        scratch_shapes=[
            pltpu.VMEM((nq, t, t), f32),
            pltpu.VMEM((nq, t, t), coarse),
            pltpu.VMEM((ATTN_HEADS, t, t), f32),
            pltpu.VMEM((ATTN_HEADS, 2 * HEAD_DIM, t), bf16),
            pltpu.VMEM((ATTN_HEADS, VT_ROWS, t), f32),
            pltpu.VMEM((ATTN_HEADS, SUBLANES, t), f32),
            pltpu.VMEM((SUBLANES, t), f32),
            pltpu.VMEM((SUBLANES, t), i32),
            pltpu.VMEM((PACK16, t), i16),
        ],
        compiler_params=pltpu.CompilerParams(
            dimension_semantics=("parallel", "arbitrary"), vmem_limit_bytes=VMEM_LIMIT),
        name="sparse_attn",
    )(qat, qit, wt, ki, ka, vt, km, vm)


def _hgrn_chunk(q, lf, k, v, st, tri, band_masks, blk_lower):
    b = jnp.dot(tri, lf, preferred_element_type=f32, precision=lax.Precision.HIGHEST) * LOG2E
    bl = b[CHUNK - 1:CHUNK, :]
    q_in = (q * jnp.exp2(b)).astype(bf16)
    o = lax.dot_general(q_in, st.astype(bf16), (((1,), (1,)), ((), ())), preferred_element_type=f32)

    blocks = [jnp.zeros((SUB, CHUNK), f32)]
    for si in range(1, CHUNK // SUB):
        bref = b[si * SUB - 1:si * SUB, :]
        rows = slice(si * SUB, (si + 1) * SUB)
        q_s = (q[rows] * jnp.exp2(b[rows] - bref)).astype(bf16)
        k_s = (k * jnp.exp2(jnp.minimum(bref - b, 0.0))).astype(bf16)
        blocks.append(lax.dot_general(q_s, k_s, (((1,), (1,)), ((), ())), preferred_element_type=f32))
    a = jnp.where(blk_lower, jnp.concatenate(blocks, axis=0), 0.0)

    for dlt in range(SUB):
        if dlt == 0:
            w = q * k
        else:
            decay = jnp.exp2(jnp.minimum(b - pltpu.roll(b, dlt, 0), 0.0))
            w = q * pltpu.roll(k, dlt, 0) * decay
        a = jnp.where(band_masks[dlt], jnp.sum(w, axis=1, keepdims=True), a)

    o = o + jnp.dot(a.astype(bf16), v.astype(bf16), preferred_element_type=f32)
    k_end = (k * jnp.exp2(bl - b)).astype(bf16)
    upd = lax.dot_general(v.astype(bf16), k_end, (((0,), (0,)), ((), ())), preferred_element_type=f32)
    return o, st * jnp.exp2(bl) + upd


def _hgrn_kernel(q_ref, lf_ref, k_ref, v_ref, lfm_ref, km_ref, vm_ref, o_ref, st_ref, *, n_chunks):
    c = pl.program_id(1)
    r_i = lax.broadcasted_iota(i32, (CHUNK, CHUNK), 0)
    c_i = lax.broadcasted_iota(i32, (CHUNK, CHUNK), 1)
    tri = jnp.where(c_i <= r_i, 1.0, 0.0).astype(f32)
    sub_shift = SUB.bit_length() - 1
    blk_lower = (c_i >> sub_shift) < (r_i >> sub_shift)
    band_masks = [((r_i & (SUB - 1)) >= d) & (c_i == r_i - d) for d in range(SUB)]

    @pl.when(c == 0)
    def _():
        pad = jnp.zeros((CHUNK - N_META, HGRN_DIM), f32)
        for h in range(HGRN_HEADS):
            cols = slice(h * HGRN_DIM, (h + 1) * HGRN_DIM)
            lf = jnp.concatenate([pad, lfm_ref[:, cols]], axis=0)
            k = jnp.concatenate([pad, km_ref[:, cols]], axis=0)
            v = jnp.concatenate([pad, vm_ref[:, cols]], axis=0)
            _, st = _hgrn_chunk(jnp.zeros((CHUNK, HGRN_DIM), f32), lf, k, v,
                                jnp.zeros((HGRN_DIM, HGRN_DIM), f32), tri, band_masks, blk_lower)
            st_ref[h] = st

    def chunk_body(ci, carry):
        rows = pl.ds(pl.multiple_of(ci * CHUNK, CHUNK), CHUNK)
        for h in range(HGRN_HEADS):
            cols = slice(h * HGRN_DIM, (h + 1) * HGRN_DIM)
            o, st = _hgrn_chunk(q_ref[rows, cols], lf_ref[rows, cols], k_ref[rows, cols],
                                v_ref[rows, cols], st_ref[h], tri, band_masks, blk_lower)
            o_ref[rows, cols] = o.astype(bf16)
            st_ref[h] = st
        return carry

    lax.fori_loop(0, n_chunks, chunk_body, 0, unroll=4)


def _hgrn(hq, hlf, hk, hv, lfm, km, vm, batch, seq, tc):
    n = hq.shape[0]
    nc = seq // tc
    row = lambda b, c: (b * nc + c, 0)
    const = lambda b, c: (0, 0)
    spec = pl.BlockSpec((tc, HGRN_WIDTH), row)
    mspec = pl.BlockSpec((N_META, HGRN_WIDTH), const)
    return pl.pallas_call(
        functools.partial(_hgrn_kernel, n_chunks=tc // CHUNK),
        grid=(batch, nc),
        in_specs=[spec, spec, spec, spec, mspec, mspec, mspec],
        out_specs=spec,
        out_shape=jax.ShapeDtypeStruct((n, HGRN_WIDTH), bf16),
        scratch_shapes=[pltpu.VMEM((HGRN_HEADS, HGRN_DIM, HGRN_DIM), f32)],
        compiler_params=pltpu.CompilerParams(
            dimension_semantics=("parallel", "arbitrary"), vmem_limit_bytes=VMEM_LIMIT),
        name="hgrn2",
    )(hq, hlf, hk, hv, lfm, km, vm)


def _outproj_kernel(oa_ref, ga_ref, oh_ref, hg_ref, x_ref, w_ref, na_ref, nh_ref, nf_ref, y_ref):
    oa = oa_ref[...].astype(f32)
    a = (oa * lax.rsqrt(jnp.mean(oa * oa, axis=-1, keepdims=True) + EPS) * na_ref[...]
         * ga_ref[...].astype(f32))
    parts = []
    for h in range(HGRN_HEADS):
        cols = slice(h * HGRN_DIM, (h + 1) * HGRN_DIM)
        oh = oh_ref[:, cols].astype(f32)
        nrm = oh * lax.rsqrt(jnp.mean(oh * oh, axis=-1, keepdims=True) + EPS) * nh_ref[...]
        parts.append(nrm * hg_ref[:, cols].astype(f32))
    hh = jnp.concatenate(parts, axis=-1)
    mix = (jnp.dot(a.astype(bf16), w_ref[:ATTN_WIDTH, :], preferred_element_type=f32)
           + jnp.dot(hh.astype(bf16), w_ref[ATTN_WIDTH:, :], preferred_element_type=f32))
    hres = x_ref[...] + mix
    y_ref[...] = hres * lax.rsqrt(jnp.mean(hres * hres, axis=-1, keepdims=True) + EPS) * nf_ref[...]


def _outproj(oa, ga, oh, hg, x2, w, na, nh, nf, tm):
    n, d_model = x2.shape
    row = lambda i: (i, 0)
    const = lambda i: (0, 0)
    half = pl.BlockSpec((tm, 512), row)
    return pl.pallas_call(
        _outproj_kernel,
        grid=(n // tm,),
        in_specs=[half, half, half, half,
                  pl.BlockSpec((tm, d_model), row),
                  pl.BlockSpec(w.shape, const),
                  pl.BlockSpec((1, ATTN_WIDTH), const),
                  pl.BlockSpec((1, HGRN_DIM), const),
                  pl.BlockSpec((1, d_model), const)],
        out_specs=pl.BlockSpec((tm, d_model), row),
        out_shape=jax.ShapeDtypeStruct((n, d_model), f32),
        compiler_params=pltpu.CompilerParams(
            dimension_semantics=("parallel",), vmem_limit_bytes=VMEM_LIMIT),
        name="outproj",
    )(oa, ga, oh, hg, x2, w, na, nh, nf)


def _rope_tables(first_pos, n_pos):
    inv = ROPE_THETA ** (-jnp.arange(ROPE_HALF, dtype=f32) * 2.0 / ROPE_DIMS)
    ang = jnp.arange(first_pos, first_pos + n_pos, dtype=f32)[:, None] * inv[None, :]
    cos, sin = jnp.cos(ang), jnp.sin(ang)
    rest = HEAD_DIM - ROPE_DIMS
    cos_h = jnp.concatenate([cos, cos, jnp.ones((n_pos, rest), f32)], axis=-1)
    sin_h = jnp.concatenate([-sin, sin, jnp.zeros((n_pos, rest), f32)], axis=-1)
    return jnp.tile(cos_h, (1, LANES // HEAD_DIM)), jnp.tile(sin_h, (1, LANES // HEAD_DIM))


def kernel(x, meta_tokens, mix_norm, w_in, idx_k_norm_g, idx_k_norm_b, attn_out_norm,
           hgrn_lb_logits, hgrn_out_norm, w_out, final_norm):
    batch, seq, d_model = x.shape
    assert seq % TILE == 0
    assert seq % HGRN_ROWS == 0
    x2 = x.reshape(batch * seq, d_model)

    wl = w_in[0]
    a_end = 4 * ATTN_WIDTH
    iq_end = a_end + IDX_HEADS * IDX_DIM
    sm_end = iq_end + IDX_DIM + IDX_HEADS
    w_small = jnp.pad(wl[:, iq_end:sm_end], ((0, 0), (0, SMALL_W - (sm_end - iq_end))))
    w = jnp.concatenate([wl[:, :iq_end], wl[:, sm_end:], w_small], axis=-1).astype(bf16)

    cos_m, sin_m = _rope_tables(0, N_META)
    cos_t, sin_t = _rope_tables(N_META, seq)
    g = mix_norm[0][None, :]
    lng = jnp.pad(idx_k_norm_g[0], (0, SMALL_W - IDX_DIM))[None, :]
    lnb = jnp.pad(idx_k_norm_b[0], (0, SMALL_W - IDX_DIM))[None, :]

    qat, ka, vt, ga, qit, ki, wt, hq, hlf, hk, hv, hg = _inproj(
        x2, g, w, cos_t, sin_t, lng, lnb, hgrn_lb_logits, seq // TILE)
    ka_m, va_m, hlf_m, hk_m, hv_m = _meta_proj(meta_tokens, g, w, cos_m, sin_m, hgrn_lb_logits)

    oa = _sparse_attention(qat, qit, wt, ki, ka, vt, ka_m, va_m, batch, seq)
    oh = _hgrn(hq, hlf, hk, hv, hlf_m, hk_m, hv_m, batch, seq, tc=HGRN_ROWS)
    y = _outproj(oa, ga, oh, hg, x2, w_out[0].astype(bf16), attn_out_norm[0][None, :],
                 hgrn_out_norm[0][None, :], final_norm[None, :], tm=256)
    return y.reshape(batch, seq, d_model)
```

```python
import functools
import math

import jax
import jax.numpy as jnp
from jax import lax
from jax.experimental import pallas as pl
from jax.experimental.pallas import tpu as pltpu

N_META = 16
ATTN_HEADS = 8
HEAD_DIM = 64
ATTN_WIDTH = ATTN_HEADS * HEAD_DIM
IDX_HEADS = 8
IDX_DIM = 64
IDX_SCALE = (IDX_HEADS ** -0.5) * (IDX_DIM ** -0.5)
ATTN_SCALE = HEAD_DIM ** -0.5
LOG2E = math.log2(math.e)
TOPK_MAX = 256
HGRN_HEADS = 4
HGRN_DIM = 128
HGRN_WIDTH = HGRN_HEADS * HGRN_DIM
CHUNK = 64
SUB = 16
ROPE_THETA = 500000.0
ROPE_DIMS = HEAD_DIM // 4
ROPE_HALF = ROPE_DIMS // 2
EPS = 1e-6

LANES = 128
SUBLANES = 8
VMEM_LIMIT = 56 * 1024 * 1024

TILE = 256
GROUP_W = 512
assert ATTN_WIDTH == GROUP_W and HGRN_WIDTH == GROUP_W
N_WIDE = 9
SMALL_W = 128
ONES_ROWS = 16
VT_ROWS = HEAD_DIM + ONES_ROWS
INT_MAX = 2 ** 31 - 1
INT_MIN = -(2 ** 31)
KEY_POS_INF = 0x7F800000
KEY_NEG_INF = -KEY_POS_INF
PACK16 = 16
FLT_LOWEST = -3.4028234663852886e38
MAX_BISECT = 18
BISECT_UNCHECKED = 12
HGRN_ROWS = 512
TILE_UNROLL = 4
BISECT_PER_CHECK = 2

f32 = jnp.float32
bf16 = jnp.bfloat16
i32 = jnp.int32
i16 = jnp.int16
coarse = jnp.bfloat16


def _sigmoid(v):
    return 1.0 / (1.0 + jnp.exp(-v))


def _normed_rows(x_ref, g_ref):
    x = x_ref[...]
    ms = jnp.mean(x * x, axis=-1, keepdims=True)
    return ((x * lax.rsqrt(ms + EPS)) * g_ref[...]).astype(bf16)


def _rope_fn(cos1, sin1, tm):
    cos = jnp.concatenate([cos1] * (ATTN_WIDTH // LANES), axis=-1)
    sin = jnp.concatenate([sin1] * (ATTN_WIDTH // LANES), axis=-1)
    lane = lax.broadcasted_iota(i32, (tm, ATTN_WIDTH), 1)
    low = (lane & (HEAD_DIM - 1)) < ROPE_HALF

    def rope(v):
        partner = jnp.where(low, pltpu.roll(v, ATTN_WIDTH - ROPE_HALF, 1),
                            pltpu.roll(v, ROPE_HALF, 1))
        return v * cos + partner * sin
    return rope


def _hgrn_gates(lbl_ref, hf):
    lg = lbl_ref[...]
    e = jnp.exp(lg - jnp.max(lg, axis=0, keepdims=True))
    lb = e[0:1, :] / jnp.sum(e, axis=0, keepdims=True)
    f = lb + (1.0 - lb) * _sigmoid(hf)
    return jnp.log(f), 1.0 - f


def _inproj_kernel(x_ref, g_ref, w_ref, cos_ref, sin_ref, lng_ref, lnb_ref, lbl_ref,
                   qat_ref, ka_ref, vt_ref, ga_ref, qit_ref, ki_ref, wt_ref,
                   hq_ref, hlf_ref, hk_ref, hv_ref, hg_ref):
    xb = _normed_rows(x_ref, g_ref)
    tm = xb.shape[0]
    cos1 = cos_ref[...]
    sin1 = sin_ref[...]
    rope = _rope_fn(cos1, sin1, tm)

    def proj(g):
        return jnp.dot(xb, w_ref[:, g * GROUP_W:(g + 1) * GROUP_W], preferred_element_type=f32)

    qat_ref[...] = (rope(proj(0)) * (ATTN_SCALE * LOG2E)).T.astype(bf16)
    ka_ref[...] = rope(proj(1)).astype(bf16)
    vt = proj(2).T
    ones = jnp.ones((ONES_ROWS, tm), f32)
    parts = []
    for h in range(ATTN_HEADS):
        parts += [vt[h * HEAD_DIM:(h + 1) * HEAD_DIM], ones]
    vt_ref[0] = jnp.concatenate(parts, axis=0).astype(bf16)
    ag = proj(3)
    ga_ref[...] = (ag * _sigmoid(ag)).astype(bf16)
    qit_ref[...] = rope(proj(4)).T.astype(bf16)

    hq = proj(5)
    hq_ref[...] = hq * _sigmoid(hq)
    hlf_ref[...], hk_ref[...] = _hgrn_gates(lbl_ref, proj(6))
    hv_ref[...] = proj(7)
    hg = proj(8)
    hg_ref[...] = (hg * _sigmoid(hg)).astype(bf16)

    sm = jnp.dot(xb, w_ref[:, N_WIDE * GROUP_W:N_WIDE * GROUP_W + SMALL_W], preferred_element_type=f32)
    lane1 = lax.broadcasted_iota(i32, (tm, SMALL_W), 1)
    is_key = lane1 < IDX_DIM
    mu = jnp.sum(jnp.where(is_key, sm, 0.0), axis=-1, keepdims=True) / IDX_DIM
    d = sm - mu
    var = jnp.sum(jnp.where(is_key, d * d, 0.0), axis=-1, keepdims=True) / IDX_DIM
    y = d * lax.rsqrt(var + EPS) * lng_ref[...] + lnb_ref[...]
    low1 = (lane1 & (IDX_DIM - 1)) < ROPE_HALF
    partner = jnp.where(low1, pltpu.roll(y, SMALL_W - ROPE_HALF, 1), pltpu.roll(y, ROPE_HALF, 1))
    y = y * cos1 + partner * sin1
    ki_ref[...] = y[:, :IDX_DIM].astype(bf16)
    wt_ref[...] = sm.T[IDX_DIM:IDX_DIM + IDX_HEADS] * IDX_SCALE


def _inproj(x2, g, w, cos_t, sin_t, lng, lnb, lbl, n_pos_blocks):
    n, d_model = x2.shape
    tm = TILE
    row = lambda i: (i, 0)
    col = lambda i: (0, i)
    const = lambda i: (0, 0)
    pos = lambda i: (i % n_pos_blocks, 0)
    wide_bf = jax.ShapeDtypeStruct((n, GROUP_W), bf16)
    wide_f = jax.ShapeDtypeStruct((n, GROUP_W), f32)
    wide_t = jax.ShapeDtypeStruct((GROUP_W, n), bf16)
    out_shape = (wide_t, wide_bf,
                 jax.ShapeDtypeStruct((n // tm, ATTN_HEADS * VT_ROWS, tm), bf16),
                 wide_bf, wide_t,
                 jax.ShapeDtypeStruct((n, IDX_DIM), bf16),
                 jax.ShapeDtypeStruct((IDX_HEADS, n), f32),
                 wide_f, wide_f, wide_f, wide_f, wide_bf)
    wide_spec = pl.BlockSpec((tm, GROUP_W), row)
    t_spec = pl.BlockSpec((GROUP_W, tm), col)
    out_specs = (t_spec, wide_spec,
                 pl.BlockSpec((1, ATTN_HEADS * VT_ROWS, tm), lambda i: (i, 0, 0)),
                 wide_spec, t_spec,
                 pl.BlockSpec((tm, IDX_DIM), row),
                 pl.BlockSpec((IDX_HEADS, tm), col)) + (wide_spec,) * 5
    return pl.pallas_call(
        _inproj_kernel,
        grid=(n // tm,),
        in_specs=[
            pl.BlockSpec((tm, d_model), row),
            pl.BlockSpec((1, d_model), const),
            pl.BlockSpec(w.shape, const),
            pl.BlockSpec((tm, LANES), pos),
            pl.BlockSpec((tm, LANES), pos),
            pl.BlockSpec((1, SMALL_W), const),
            pl.BlockSpec((1, SMALL_W), const),
            pl.BlockSpec(lbl.shape, const),
        ],
        out_specs=out_specs,
        out_shape=out_shape,
        compiler_params=pltpu.CompilerParams(
            dimension_semantics=("parallel",), vmem_limit_bytes=VMEM_LIMIT),
        name="inproj",
    )(x2, g, w, cos_t, sin_t, lng, lnb, lbl)


def _meta_proj_kernel(x_ref, g_ref, w_ref, cos_ref, sin_ref, lbl_ref,
                      ka_ref, va_ref, hlf_ref, hk_ref, hv_ref):
    xb = _normed_rows(x_ref, g_ref)
    rope = _rope_fn(cos_ref[...], sin_ref[...], xb.shape[0])

    def proj(g):
        return jnp.dot(xb, w_ref[:, g * GROUP_W:(g + 1) * GROUP_W], preferred_element_type=f32)

    ka_ref[...] = rope(proj(1)).astype(bf16)
    va_ref[...] = proj(2).astype(bf16)
    hlf_ref[...], hk_ref[...] = _hgrn_gates(lbl_ref, proj(6))
    hv_ref[...] = proj(7)


def _meta_proj(meta, g, w, cos_t, sin_t, lbl):
    n, d_model = meta.shape
    full = lambda a: pl.BlockSpec(a.shape, lambda i: (0,) * a.ndim)
    wide = pl.BlockSpec((n, GROUP_W), lambda i: (0, 0))
    return pl.pallas_call(
        _meta_proj_kernel,
        grid=(1,),
        in_specs=[full(meta), full(g), full(w), full(cos_t), full(sin_t), full(lbl)],
        out_specs=(wide,) * 5,
        out_shape=(jax.ShapeDtypeStruct((n, GROUP_W), bf16),) * 2
        + (jax.ShapeDtypeStruct((n, GROUP_W), f32),) * 3,
        compiler_params=pltpu.CompilerParams(vmem_limit_bytes=VMEM_LIMIT),
        name="meta_proj",
    )(meta, g, w, cos_t, sin_t, lbl)


def _unkey(k):
    return lax.bitcast_convert_type(jnp.where(k < 0, INT_MIN - k, k), f32)


def _for_each_tile(n, body, merged=False):
    def run(start, count):
        if merged:
            body(start, count)
        else:
            for u in range(count):
                body(start + u, 1)

    def trip(j, carry):
        run(TILE_UNROLL * j, TILE_UNROLL)
        return carry

    lax.fori_loop(0, n // TILE_UNROLL, trip, 0)
    group = TILE_UNROLL // 2
    while group >= 1:
        def rest(group=group):
            run((n // (2 * group)) * (2 * group), group)
        pl.when((n & group) != 0)(rest)
        group //= 2


def _sum_tiles(n, tile_fn, acc_ref):
    z = jnp.zeros(acc_ref.shape, acc_ref.dtype)

    def trip(j, accs):
        return tile_fn(2 * j + 1, tile_fn(2 * j, accs))

    a = lax.fori_loop(0, n // 2, trip, (z, z, z, z))
    acc_ref[...] = (a[0] + a[1]) + (a[2] + a[3])

    @pl.when((n & 1) != 0)
    def _():
        b = tile_fn(n - 1, (acc_ref[...], z, z, z))
        acc_ref[...] = (b[0] + b[1]) + (b[2] + b[3])

    return acc_ref[...]


def _attn_kernel(qat_ref, qit_ref, wt_ref, ki_ref, ka_ref, vt_ref, km_ref, vm_ref, o_ref,
                 sc_ref, sb_ref, s_ref, qm_ref, acc_ref, m_ref, thr_ref, cnt_ref, cnt16_ref,
                 *, t, topk):
    i = pl.program_id(1)
    nkb = i + 1
    key_l = lax.broadcasted_iota(i32, (t, t), 0)
    qry_l = lax.broadcasted_iota(i32, (t, t), 1)

    def score_tile(kb, causal):
        kik = ki_ref[pl.ds(pl.multiple_of(kb * t, t), t), :]
        acc = jnp.zeros((t, t), f32)
        for h in range(IDX_HEADS):
            s = jnp.dot(kik, qit_ref[h * IDX_DIM:(h + 1) * IDX_DIM, :], preferred_element_type=f32)
            acc = acc + jnp.maximum(s, 0.0) * wt_ref[h:h + 1, :]
        if causal:
            acc = jnp.where(key_l <= qry_l, acc, -jnp.inf)
        sc_ref[kb] = acc
        sb_ref[kb] = acc.astype(coarse)

    _for_each_tile(i, lambda kb, count: score_tile(kb, False))
    score_tile(i, True)

    sub_l = lax.broadcasted_iota(i32, (SUBLANES, t), 0)

    def count_ge(cand):
        cb = jnp.broadcast_to(cand, (SUBLANES, t))

        def tile(kb, accs):
            accs = list(accs)
            for r in range(t // SUBLANES):
                v = sc_ref[kb, r * SUBLANES:(r + 1) * SUBLANES, :]
                accs[r % 4] = accs[r % 4] + jnp.where(v >= cb, 1, 0)
            return tuple(accs)
        return jnp.sum(_sum_tiles(nkb, tile, cnt_ref), axis=0, keepdims=True)

    def count_ge_coarse(cand):
        cb = jnp.broadcast_to(cand, (PACK16, t)).astype(coarse)
        one, zero = jnp.int16(1), jnp.int16(0)

        def tile(kb, accs):
            accs = list(accs)
            for r in range(t // PACK16):
                v = sb_ref[kb, r * PACK16:(r + 1) * PACK16, :]
                accs[r % 4] = accs[r % 4] + jnp.where(v >= cb, one, zero)
            return tuple(accs)
        return jnp.sum(_sum_tiles(nkb, tile, cnt16_ref).astype(i32), axis=0, keepdims=True)

    def bisect(count_fn, to_float, carry, below_hi=False):
        lo, hi, c_lo, c_hi = carry
        mid = hi - 1 if below_hi else lo + lax.shift_right_logical(hi - lo, 1)
        c = count_fn(to_float(mid))
        up = (c >= topk) & (hi != lo + 1)
        dn = (c < topk) & (hi != lo + 1)
        return (jnp.where(up, mid, lo), jnp.where(dn, mid, hi),
                jnp.where(up, c, c_lo), jnp.where(dn, c, c_hi))

    row = lambda val: jnp.full((1, t), val, i32)
    top_lo, top_hi = KEY_NEG_INF >> 16, KEY_POS_INF >> 16
    top, _, _, _ = lax.fori_loop(
        0, (top_hi - top_lo - 1).bit_length(),
        lambda j, c: bisect(count_ge_coarse, lambda m: _unkey(m * (1 << 16)), c),
        (row(top_lo), row(top_hi), row(0) + nkb * t, row(0)))

    lo0 = jnp.maximum(top - 1, top_lo) * (1 << 16)
    hi0 = jnp.minimum(top + 1, top_hi) * (1 << 16)
    c_hi0 = count_ge(_unkey(hi0))

    def cond(carry):
        it, lo, hi, c_lo, c_hi = carry
        open_ = jnp.where((c_lo != topk) & (hi != lo + 1), 1.0, 0.0)
        return (it < MAX_BISECT) & (jnp.max(open_) > 0.0)

    def step(carry):
        it, state = carry[0], carry[1:]
        for _ in range(BISECT_PER_CHECK):
            state = bisect(count_ge, _unkey, state)
        return (it + BISECT_PER_CHECK,) + state

    state = bisect(count_ge, _unkey, (lo0, hi0, row(INT_MAX), c_hi0), below_hi=True)
    state = lax.fori_loop(0, BISECT_UNCHECKED, lambda j, st: bisect(count_ge, _unkey, st), state)
    _, lo, hi, c_lo, c_hi = lax.while_loop(cond, step, (jnp.int32(BISECT_UNCHECKED),) + state)
    thr = _unkey(lo)
    thr_ref[...] = jnp.broadcast_to(jnp.maximum(thr, FLT_LOWEST), (SUBLANES, t))

    @pl.when(jnp.max(c_lo) > topk)
    def _():
        tb = jnp.broadcast_to(thr, (SUBLANES, t))
        need = topk - c_hi
        zero = jnp.zeros((1, t), i32)

        def tied_in(ref_tile, bound):
            z = jnp.zeros((SUBLANES, t), i32)
            accs = [z, z, z, z]
            for r in range(t // SUBLANES):
                v = ref_tile[r * SUBLANES:(r + 1) * SUBLANES, :]
                hit = v == tb
                if bound is not None:
                    hit = hit & (r * SUBLANES + sub_l < bound)
                accs[r % 4] = accs[r % 4] + jnp.where(hit, 1, 0)
            return jnp.sum((accs[0] + accs[1]) + (accs[2] + accs[3]), axis=0, keepdims=True)

        def locate(kb, carry):
            seen, tile_of, before = carry
            new = seen + tied_in(sc_ref.at[kb], None)
            cross = (seen < need) & (new >= need)
            return new, jnp.where(cross, kb, tile_of), jnp.where(cross, seen, before)

        _, tile_of, before = lax.fori_loop(0, nkb, locate, (zero, zero, zero))
        need_in = need - before

        g_ref = s_ref.at[0, 0]
        g_ref[...] = sc_ref[0]

        def gather(kb, carry):
            g_ref[...] = jnp.where(tile_of == kb, sc_ref[kb], g_ref[...])
            return carry

        lax.fori_loop(1, nkb, gather, 0)

        tile_bits = (t - 1).bit_length()

        def idx_body(j, jcol):
            cand = jcol | (1 << (tile_bits - 1 - j))
            c = tied_in(g_ref, jnp.broadcast_to(cand, (SUBLANES, t)))
            return jnp.where(c < need_in, cand, jcol)

        jcol = tile_of * t + lax.fori_loop(0, tile_bits, idx_body, zero)
        jcol = jnp.where(c_lo > topk, jcol, INT_MAX)

        def drop_body(kb, carry):
            v = sc_ref[kb]
            sc_ref[kb] = jnp.where((v == thr) & (kb * t + key_l > jcol), -jnp.inf, v)
            return carry

        lax.fori_loop(0, nkb, drop_body, 0)


    row_h = lax.broadcasted_iota(i32, (2 * HEAD_DIM, t), 0)
    first = row_h < HEAD_DIM

    def q_head(h):
        return qm_ref[h]

    for h in range(ATTN_HEADS):
        cols = slice((h // 2) * LANES, (h // 2 + 1) * LANES)
        qp = qat_ref[cols, :]
        qm_ref[h] = jnp.where(first if h % 2 == 0 else ~first, qp, jnp.zeros_like(qp))
        s = jnp.dot(km_ref[:, cols], q_head(h), preferred_element_type=f32)
        m0 = jnp.max(s, axis=0, keepdims=True)
        p = jnp.exp2(s - m0).astype(bf16)
        pv = lax.dot_general(vm_ref[:, cols], p, (((0,), (0,)), ((), ())), preferred_element_type=f32)
        pv = pv[(h % 2) * HEAD_DIM:(h % 2 + 1) * HEAD_DIM]
        l0 = jnp.sum(p.astype(f32), axis=0, keepdims=True)
        acc_ref[h] = jnp.concatenate([pv, jnp.broadcast_to(l0, (ONES_ROWS, t))], axis=0)
        m_ref[h] = jnp.broadcast_to(m0, (SUBLANES, t))

    thr_b = thr_ref[0:1, :]

    def attn_tiles(kb, n_tiles):
        biases = [jnp.where(sc_ref[kb + u] >= thr_b, 0.0, -jnp.inf) for u in range(n_tiles)]
        alphas, m_news = [], []
        for h in range(ATTN_HEADS):
            cols = slice((h // 2) * LANES, (h // 2 + 1) * LANES)
            m_old = m_ref[h][0:1, :]
            m_new = m_old
            for u in range(n_tiles):
                r0 = pl.multiple_of((kb + u) * t, t)
                s = jnp.dot(ka_ref[pl.ds(r0, t), cols], q_head(h), preferred_element_type=f32) + biases[u]
                s_ref[u, h] = s
                m_new = jnp.maximum(m_new, jnp.max(s, axis=0, keepdims=True))
            m_ref[h] = jnp.broadcast_to(m_new, (SUBLANES, t))
            alphas.append(jnp.exp2(m_old - m_new))
            m_news.append(m_new)
        for h in range(ATTN_HEADS):
            pv = None
            for u in range(n_tiles):
                p = jnp.exp2(s_ref[u, h] - m_news[h]).astype(bf16)
                d = jnp.dot(vt_ref[kb + u, h * VT_ROWS:(h + 1) * VT_ROWS, :], p, preferred_element_type=f32)
                pv = d if pv is None else pv + d
            acc_ref[h] = alphas[h] * acc_ref[h] + pv

    _for_each_tile(nkb, attn_tiles, merged=True)

    outs = []
    for h in range(ATTN_HEADS):
        a = acc_ref[h]
        outs.append(a[:HEAD_DIM] / a[HEAD_DIM:HEAD_DIM + 1])
    o_ref[...] = jnp.concatenate(outs, axis=0).T.astype(bf16)


def _sparse_attention(qat, qit, wt, ki, ka, vt, km, vm, batch, seq):
    t = TILE
    n = ka.shape[0]
    nq = seq // t
    topk = min(TOPK_MAX, seq // 4)
    qcol = lambda b, i: (0, b * nq + i)
    brow = lambda b, i: (b, 0)
    const = lambda b, i: (0, 0)
    once = pl.Buffered(1)
    return pl.pallas_call(
        functools.partial(_attn_kernel, t=t, topk=topk),
        grid=(batch, nq),
        in_specs=[
            pl.BlockSpec((ATTN_WIDTH, t), qcol),
            pl.BlockSpec((ATTN_WIDTH, t), qcol),
            pl.BlockSpec((IDX_HEADS, t), qcol),
            pl.BlockSpec((seq, IDX_DIM), brow, pipeline_mode=once),
            pl.BlockSpec((seq, ATTN_WIDTH), brow, pipeline_mode=once),
            pl.BlockSpec((nq, ATTN_HEADS * VT_ROWS, t), lambda b, i: (b, 0, 0), pipeline_mode=once),
            pl.BlockSpec((N_META, ATTN_WIDTH), const),
            pl.BlockSpec((N_META, ATTN_WIDTH), const),
        ],
        out_specs=pl.BlockSpec((t, ATTN_WIDTH), lambda b, i: (b * nq + i, 0)),
        out_shape=jax.ShapeDtypeStruct((n, ATTN_WIDTH), bf16),
        scratch_shapes=[
            pltpu.VMEM((nq, t, t), f32),
            pltpu.VMEM((nq, t, t), coarse),
            pltpu.VMEM((TILE_UNROLL, ATTN_HEADS, t, t), f32),
            pltpu.VMEM((ATTN_HEADS, 2 * HEAD_DIM, t), bf16),
            pltpu.VMEM((ATTN_HEADS, VT_ROWS, t), f32),
            pltpu.VMEM((ATTN_HEADS, SUBLANES, t), f32),
            pltpu.VMEM((SUBLANES, t), f32),
            pltpu.VMEM((SUBLANES, t), i32),
            pltpu.VMEM((PACK16, t), i16),
        ],
        compiler_params=pltpu.CompilerParams(
            dimension_semantics=("parallel", "arbitrary"), vmem_limit_bytes=VMEM_LIMIT),
        name="sparse_attn",
    )(qat, qit, wt, ki, ka, vt, km, vm)


def _hgrn_chunk(q, lf, k, v, st, tri, band_masks, blk_lower):
    b = jnp.dot(tri, lf, preferred_element_type=f32, precision=lax.Precision.HIGHEST) * LOG2E
    bl = b[CHUNK - 1:CHUNK, :]
    q_in = (q * jnp.exp2(b)).astype(bf16)
    o = lax.dot_general(q_in, st.astype(bf16), (((1,), (1,)), ((), ())), preferred_element_type=f32)

    blocks = [jnp.zeros((SUB, CHUNK), f32)]
    for si in range(1, CHUNK // SUB):
        bref = b[si * SUB - 1:si * SUB, :]
        rows = slice(si * SUB, (si + 1) * SUB)
        q_s = (q[rows] * jnp.exp2(b[rows] - bref)).astype(bf16)
        k_s = (k * jnp.exp2(jnp.minimum(bref - b, 0.0))).astype(bf16)
        blocks.append(lax.dot_general(q_s, k_s, (((1,), (1,)), ((), ())), preferred_element_type=f32))
    a = jnp.where(blk_lower, jnp.concatenate(blocks, axis=0), 0.0)

    for dlt in range(SUB):
        if dlt == 0:
            w = q * k
        else:
            decay = jnp.exp2(jnp.minimum(b - pltpu.roll(b, dlt, 0), 0.0))
            w = q * pltpu.roll(k, dlt, 0) * decay
        a = jnp.where(band_masks[dlt], jnp.sum(w, axis=1, keepdims=True), a)

    o = o + jnp.dot(a.astype(bf16), v.astype(bf16), preferred_element_type=f32)
    k_end = (k * jnp.exp2(bl - b)).astype(bf16)
    upd = lax.dot_general(v.astype(bf16), k_end, (((0,), (0,)), ((), ())), preferred_element_type=f32)
    return o, st * jnp.exp2(bl) + upd


def _hgrn_kernel(q_ref, lf_ref, k_ref, v_ref, lfm_ref, km_ref, vm_ref, o_ref, st_ref, *, n_chunks):
    c = pl.program_id(1)
    r_i = lax.broadcasted_iota(i32, (CHUNK, CHUNK), 0)
    c_i = lax.broadcasted_iota(i32, (CHUNK, CHUNK), 1)
    tri = jnp.where(c_i <= r_i, 1.0, 0.0).astype(f32)
    sub_shift = SUB.bit_length() - 1
    blk_lower = (c_i >> sub_shift) < (r_i >> sub_shift)
    band_masks = [((r_i & (SUB - 1)) >= d) & (c_i == r_i - d) for d in range(SUB)]

    @pl.when(c == 0)
    def _():
        pad = jnp.zeros((CHUNK - N_META, HGRN_DIM), f32)
        for h in range(HGRN_HEADS):
            cols = slice(h * HGRN_DIM, (h + 1) * HGRN_DIM)
            lf = jnp.concatenate([pad, lfm_ref[:, cols]], axis=0)
            k = jnp.concatenate([pad, km_ref[:, cols]], axis=0)
            v = jnp.concatenate([pad, vm_ref[:, cols]], axis=0)
            _, st = _hgrn_chunk(jnp.zeros((CHUNK, HGRN_DIM), f32), lf, k, v,
                                jnp.zeros((HGRN_DIM, HGRN_DIM), f32), tri, band_masks, blk_lower)
            st_ref[h] = st

    def chunk_body(ci, carry):
        rows = pl.ds(pl.multiple_of(ci * CHUNK, CHUNK), CHUNK)
        for h in range(HGRN_HEADS):
            cols = slice(h * HGRN_DIM, (h + 1) * HGRN_DIM)
            o, st = _hgrn_chunk(q_ref[rows, cols], lf_ref[rows, cols], k_ref[rows, cols],
                                v_ref[rows, cols], st_ref[h], tri, band_masks, blk_lower)
            o_ref[rows, cols] = o.astype(bf16)
            st_ref[h] = st
        return carry

    lax.fori_loop(0, n_chunks, chunk_body, 0, unroll=4)


def _hgrn(hq, hlf, hk, hv, lfm, km, vm, batch, seq, tc):
    n = hq.shape[0]
    nc = seq // tc
    row = lambda b, c: (b * nc + c, 0)
    const = lambda b, c: (0, 0)
    spec = pl.BlockSpec((tc, HGRN_WIDTH), row)
    mspec = pl.BlockSpec((N_META, HGRN_WIDTH), const)
    return pl.pallas_call(
        functools.partial(_hgrn_kernel, n_chunks=tc // CHUNK),
        grid=(batch, nc),
        in_specs=[spec, spec, spec, spec, mspec, mspec, mspec],
        out_specs=spec,
        out_shape=jax.ShapeDtypeStruct((n, HGRN_WIDTH), bf16),
        scratch_shapes=[pltpu.VMEM((HGRN_HEADS, HGRN_DIM, HGRN_DIM), f32)],
        compiler_params=pltpu.CompilerParams(
            dimension_semantics=("parallel", "arbitrary"), vmem_limit_bytes=VMEM_LIMIT),
        name="hgrn2",
    )(hq, hlf, hk, hv, lfm, km, vm)


def _outproj_kernel(oa_ref, ga_ref, oh_ref, hg_ref, x_ref, w_ref, na_ref, nh_ref, nf_ref, y_ref):
    oa = oa_ref[...].astype(f32)
    a = (oa * lax.rsqrt(jnp.mean(oa * oa, axis=-1, keepdims=True) + EPS) * na_ref[...]
         * ga_ref[...].astype(f32))
    parts = []
    for h in range(HGRN_HEADS):
        cols = slice(h * HGRN_DIM, (h + 1) * HGRN_DIM)
        oh = oh_ref[:, cols].astype(f32)
        nrm = oh * lax.rsqrt(jnp.mean(oh * oh, axis=-1, keepdims=True) + EPS) * nh_ref[...]
        parts.append(nrm * hg_ref[:, cols].astype(f32))
    hh = jnp.concatenate(parts, axis=-1)
    mix = (jnp.dot(a.astype(bf16), w_ref[:ATTN_WIDTH, :], preferred_element_type=f32)
           + jnp.dot(hh.astype(bf16), w_ref[ATTN_WIDTH:, :], preferred_element_type=f32))
    hres = x_ref[...] + mix
    y_ref[...] = hres * lax.rsqrt(jnp.mean(hres * hres, axis=-1, keepdims=True) + EPS) * nf_ref[...]


def _outproj(oa, ga, oh, hg, x2, w, na, nh, nf, tm):
    n, d_model = x2.shape
    row = lambda i: (i, 0)
    const = lambda i: (0, 0)
    half = pl.BlockSpec((tm, GROUP_W), row)
    return pl.pallas_call(
        _outproj_kernel,
        grid=(n // tm,),
        in_specs=[half, half, half, half,
                  pl.BlockSpec((tm, d_model), row),
                  pl.BlockSpec(w.shape, const),
                  pl.BlockSpec((1, ATTN_WIDTH), const),
                  pl.BlockSpec((1, HGRN_DIM), const),
                  pl.BlockSpec((1, d_model), const)],
        out_specs=pl.BlockSpec((tm, d_model), row),
        out_shape=jax.ShapeDtypeStruct((n, d_model), f32),
        compiler_params=pltpu.CompilerParams(
            dimension_semantics=("parallel",), vmem_limit_bytes=VMEM_LIMIT),
        name="outproj",
    )(oa, ga, oh, hg, x2, w, na, nh, nf)


def _rope_tables(first_pos, n_pos):
    inv = ROPE_THETA ** (-jnp.arange(ROPE_HALF, dtype=f32) * 2.0 / ROPE_DIMS)
    ang = jnp.arange(first_pos, first_pos + n_pos, dtype=f32)[:, None] * inv[None, :]
    cos, sin = jnp.cos(ang), jnp.sin(ang)
    rest = HEAD_DIM - ROPE_DIMS
    cos_h = jnp.concatenate([cos, cos, jnp.ones((n_pos, rest), f32)], axis=-1)
    sin_h = jnp.concatenate([-sin, sin, jnp.zeros((n_pos, rest), f32)], axis=-1)
    return jnp.tile(cos_h, (1, LANES // HEAD_DIM)), jnp.tile(sin_h, (1, LANES // HEAD_DIM))


def kernel(x, meta_tokens, mix_norm, w_in, idx_k_norm_g, idx_k_norm_b, attn_out_norm,
           hgrn_lb_logits, hgrn_out_norm, w_out, final_norm):
    batch, seq, d_model = x.shape
    assert seq % TILE == 0
    assert seq % HGRN_ROWS == 0
    x2 = x.reshape(batch * seq, d_model)

    wl = w_in[0]
    a_end = 4 * ATTN_WIDTH
    iq_end = a_end + IDX_HEADS * IDX_DIM
    sm_end = iq_end + IDX_DIM + IDX_HEADS
    w_small = jnp.pad(wl[:, iq_end:sm_end], ((0, 0), (0, SMALL_W - (sm_end - iq_end))))
    w = jnp.concatenate([wl[:, :iq_end], wl[:, sm_end:], w_small], axis=-1).astype(bf16)

    cos_m, sin_m = _rope_tables(0, N_META)
    cos_t, sin_t = _rope_tables(N_META, seq)
    g = mix_norm[0][None, :]
    lng = jnp.pad(idx_k_norm_g[0], (0, SMALL_W - IDX_DIM))[None, :]
    lnb = jnp.pad(idx_k_norm_b[0], (0, SMALL_W - IDX_DIM))[None, :]

    qat, ka, vt, ga, qit, ki, wt, hq, hlf, hk, hv, hg = _inproj(
        x2, g, w, cos_t, sin_t, lng, lnb, hgrn_lb_logits, seq // TILE)
    ka_m, va_m, hlf_m, hk_m, hv_m = _meta_proj(meta_tokens, g, w, cos_m, sin_m, hgrn_lb_logits)

    oa = _sparse_attention(qat, qit, wt, ki, ka, vt, ka_m, va_m, batch, seq)
    oh = _hgrn(hq, hlf, hk, hv, hlf_m, hk_m, hv_m, batch, seq, tc=HGRN_ROWS)
    y = _outproj(oa, ga, oh, hg, x2, w_out[0].astype(bf16), attn_out_norm[0][None, :],
                 hgrn_out_norm[0][None, :], final_norm[None, :], tm=TILE)
    return y.reshape(batch, seq, d_model)
```
